```python
import math
import jax, jax.numpy as jnp
from jax import lax
import numpy as np

D_MODEL = 4096
BATCH = 2
SEQ = 4096
DEPTH = 2

N_META = 16
LEAD = 128
N_PAD = LEAD - N_META
HEAD_DIM = 128
ROT_DIM = HEAD_DIM // 4
ROPE_THETA = 500000.0
CONV_W = 7
EPS = 1e-6

GDN_HEADS = D_MODEL // 256
GDN_DK = 128
GDN_DV = 128
GDN_QK = GDN_HEADS * GDN_DK
GDN_WIDTH = GDN_HEADS * GDN_DV
GDN_CHUNK = 64
DIFF_HEADS = D_MODEL // 512
DIFF_DQK = 128
DIFF_DV = 2 * DIFF_DQK
DIFF_WIDTH = DIFF_HEADS * DIFF_DV
DIFF_BLOCK = 128
SWA_HEADS = D_MODEL // 256
SWA_KV_HEADS = SWA_HEADS // 4
SWA_GROUP = SWA_HEADS // SWA_KV_HEADS
SWA_WIDTH = SWA_HEADS * HEAD_DIM
SWA_WINDOW = 128
SWA_BLOCK = 128
SSD_WIDTH = D_MODEL // 2
SSD_HEADDIM = 64
SSD_HEADS = SSD_WIDTH // SSD_HEADDIM
SSD_STATE = 128
SSD_GROUPS = 4
SSD_XBC = SSD_WIDTH + 2 * SSD_GROUPS * SSD_STATE
SSD_CHUNK = 128
D_FF = ((8 * D_MODEL + 3 * 256 - 1) // (3 * 256)) * 256

N_EVEN = (DEPTH + 1) // 2
N_ODD = DEPTH // 2
EVEN_IN = 2 * GDN_QK + GDN_WIDTH + GDN_WIDTH + 4 * GDN_HEADS + 2 * DIFF_HEADS * 2 * DIFF_DQK + DIFF_WIDTH
ODD_IN = SWA_WIDTH + 2 * SWA_KV_HEADS * HEAD_DIM + SSD_WIDTH + SSD_XBC + 2 * SSD_HEADS
EVEN_MIX = GDN_WIDTH + DIFF_WIDTH
ODD_MIX = SWA_WIDTH + SSD_WIDTH

kernel_name = 'hybrid_meta_gdn_diff_swa_ssd_encoder'

F32 = jnp.float32


def rmsnorm(x, w, eps=EPS):
    xf = x.astype(F32)
    y = xf * lax.rsqrt(jnp.mean(xf * xf, axis=-1, keepdims=True) + eps)
    return (y * w.astype(F32)).astype(x.dtype)


def l2norm(t):
    tf = t.astype(F32)
    return (tf * lax.rsqrt(jnp.sum(tf * tf, axis=-1, keepdims=True) + 1e-6)).astype(t.dtype)


def split_cols(t, sizes):
    out, off = [], 0
    for s in sizes:
        out.append(t[..., off:off + s])
        off += s
    return out


def lead_pad(t):
    return jnp.pad(t, [(0, 0), (N_PAD, 0)] + [(0, 0)] * (t.ndim - 2))


def rev(t):
    return jnp.flip(t, axis=1)


def rope_partial(x, pos):
    half = ROT_DIM // 2
    inv = ROPE_THETA ** (-jnp.arange(0, ROT_DIM, 2, dtype=F32) / ROT_DIM)
    ang = pos.astype(F32)[:, None] * inv[None, :]
    cos = jnp.cos(ang)[None, :, None, :]
    sin = jnp.sin(ang)[None, :, None, :]
    xr = x[..., :ROT_DIM].astype(F32)
    x1, x2 = xr[..., :half], xr[..., half:]
    rot = jnp.concatenate([x1 * cos - x2 * sin, x2 * cos + x1 * sin], axis=-1)
    return jnp.concatenate([rot.astype(x.dtype), x[..., ROT_DIM:]], axis=-1)


def dwconv_centred(x, w):
    c = x.shape[-1]
    return lax.conv_general_dilated(x, w[:, None, :].astype(x.dtype), window_strides=(1,),
                                    padding=[(CONV_W // 2, CONV_W // 2)],
                                    dimension_numbers=('NWC', 'WIO', 'NWC'), feature_group_count=c)


def gated_delta_rule_chunked(q, k, v, beta, g):
    out_dtype = v.dtype
    bsz, t_len, n_h, dk = q.shape
    dv = v.shape[-1]
    c = GDN_CHUNK
    n = t_len // c

    def chunk(t):
        t = t.astype(F32).reshape((bsz, n, c, n_h) + t.shape[3:])
        return jnp.moveaxis(t, 2, 3)

    q, k, v, beta, g = chunk(q), chunk(k), chunk(v), chunk(beta), chunk(g)
    gc = jnp.cumsum(g, axis=-1)
    incl = jnp.tril(jnp.ones((c, c), dtype=bool))
    strict = jnp.tril(jnp.ones((c, c), dtype=bool), -1)
    decay = jnp.where(incl, jnp.exp(jnp.where(incl, gc[..., :, None] - gc[..., None, :], 0.0)), 0.0)
    kb = k * beta[..., None]
    lower = jnp.where(strict, jnp.einsum('bnhid,bnhjd->bnhij', kb, k) * decay, 0.0)
    a_mat = jnp.eye(c, dtype=F32) + lower
    u = lax.linalg.triangular_solve(a_mat, v * beta[..., None], left_side=True, lower=True, unit_diagonal=True)
    w = lax.linalg.triangular_solve(a_mat, kb * jnp.exp(gc)[..., None], left_side=True, lower=True, unit_diagonal=True)
    qk = jnp.einsum('bnhid,bnhjd->bnhij', q, k) * decay
    q_dec = q * jnp.exp(gc)[..., None]
    k_dec = k * jnp.exp(gc[..., -1:] - gc)[..., None]
    c_dec = jnp.exp(gc[..., -1])

    def step(s, inp):
        u_n, w_n, qk_n, qd_n, kd_n, cd_n = inp
        v_new = u_n - jnp.einsum('bhcd,bhde->bhce', w_n, s)
        o = jnp.einsum('bhcd,bhde->bhce', qd_n, s) + jnp.einsum('bhij,bhje->bhie', qk_n, v_new)
        s = s * cd_n[..., None, None] + jnp.einsum('bhcd,bhce->bhde', kd_n, v_new)
        return s, o

    xs = tuple(jnp.moveaxis(t, 1, 0) for t in (u, w, qk, q_dec, k_dec, c_dec))
    s0 = jnp.zeros((bsz, n_h, dk, dv), F32)
    _, o = lax.scan(step, s0, xs)
    return jnp.transpose(o, (1, 0, 3, 2, 4)).reshape(bsz, t_len, n_h, dv).astype(out_dtype)


def ssd_chunked(x, dt, a, b_in, c_in):
    out_dtype = x.dtype
    bsz, t_len, n_h, p = x.shape
    n_g, n_st = b_in.shape[2], b_in.shape[3]
    q = SSD_CHUNK
    nc = t_len // q
    xdt = (x.astype(F32) * dt[..., None]).reshape(bsz, nc, q, n_h, p)
    bh = jnp.repeat(b_in.astype(F32), n_h // n_g, axis=2).reshape(bsz, nc, q, n_h, n_st)
    ch = jnp.repeat(c_in.astype(F32), n_h // n_g, axis=2).reshape(bsz, nc, q, n_h, n_st)
    acum = jnp.cumsum((dt * a).reshape(bsz, nc, q, n_h), axis=2)
    incl = jnp.tril(jnp.ones((q, q), dtype=bool))[:, :, None]
    seg = jnp.where(incl, jnp.exp(jnp.where(incl, acum[:, :, :, None, :] - acum[:, :, None, :, :], 0.0)), 0.0)
    scores = jnp.einsum('bclhn,bcshn->bclsh', ch, bh) * seg
    y_diag = jnp.einsum('bclsh,bcshp->bclhp', scores, xdt)
    decay_states = jnp.exp(acum[:, :, -1:, :] - acum)
    states = jnp.einsum('bclhn,bclhp->bchpn', bh * decay_states[..., None], xdt)
    chunk_decay = jnp.exp(acum[:, :, -1, :])

    def step(s, inp):
        st, cd = inp
        return s * cd[:, :, None, None] + st, s

    s0 = jnp.zeros((bsz, n_h, p, n_st), F32)
    _, prev = lax.scan(step, s0, (jnp.moveaxis(states, 1, 0), jnp.moveaxis(chunk_decay, 1, 0)))
    prev = jnp.moveaxis(prev, 0, 1)
    y_off = jnp.einsum('bclhn,bchpn->bclhp', ch * jnp.exp(acum)[..., None], prev)
    return (y_diag + y_off).reshape(bsz, t_len, n_h, p).astype(out_dtype)


def diff_attention_blocks(q, k, v, lam):
    bsz, t_len, n_h, _, d = q.shape
    nb = t_len // DIFF_BLOCK
    key_ok = jnp.arange(t_len) >= N_PAD
    scale = d ** -0.5
    qb = jnp.moveaxis(q.reshape(bsz, nb, DIFF_BLOCK, n_h, 2, d), 1, 0)

    def block(qblk):
        s = jnp.einsum('bqhmd,bkhmd->bhmqk', qblk, k, preferred_element_type=F32) * scale
        p = jax.nn.softmax(jnp.where(key_ok, s, -jnp.inf), axis=-1)
        wts = p[:, :, 0] - lam * p[:, :, 1]
        return jnp.einsum('bhqk,bkhe->bqhe', wts.astype(v.dtype), v)

    o = lax.map(block, qb)
    return jnp.moveaxis(o, 0, 1).reshape(bsz, t_len, n_h, v.shape[-1])


def window_attention_blocks(q, k, v, sink):
    bsz, t_len, n_kv, n_grp, d = q.shape
    nb = t_len // SWA_BLOCK
    scale = d ** -0.5
    qb = q.reshape(bsz, nb, SWA_BLOCK, n_kv, n_grp, d)

    def band(t):
        tb = t.reshape(bsz, nb, SWA_BLOCK, n_kv, d)
        z = jnp.zeros_like(tb[:, :1])
        return jnp.concatenate([jnp.concatenate([z, tb[:, :-1]], 1), tb, jnp.concatenate([tb[:, 1:], z], 1)], axis=2)

    kband, vband = band(k), band(v)
    blk = jnp.arange(nb)[:, None]
    qpos = blk * SWA_BLOCK + jnp.arange(SWA_BLOCK)[None, :]
    kpos = (blk - 1) * SWA_BLOCK + jnp.arange(3 * SWA_BLOCK)[None, :]
    band_ok = ((jnp.abs(qpos[:, :, None] - kpos[:, None, :]) <= SWA_WINDOW)
               & (kpos[:, None, :] >= LEAD) & (kpos[:, None, :] < t_len))
    k_meta, v_meta = k[:, N_PAD:LEAD], v[:, N_PAD:LEAD]
    s_band = jnp.einsum('bnqhgd,bnkhd->bnhgqk', qb, kband, preferred_element_type=F32) * scale
    s_band = jnp.where(band_ok[None, :, None, None], s_band, -jnp.inf)
    s_meta = jnp.einsum('bnqhgd,bmhd->bnhgqm', qb, k_meta, preferred_element_type=F32) * scale
    s_sink = jnp.broadcast_to(sink.astype(F32).reshape(1, 1, n_kv, n_grp, 1, 1), s_meta.shape[:-1] + (1,))
    p = jax.nn.softmax(jnp.concatenate([s_band, s_meta, s_sink], axis=-1), axis=-1).astype(v.dtype)
    nk = 3 * SWA_BLOCK
    o = (jnp.einsum('bnhgqk,bnkhd->bnqhgd', p[..., :nk], vband)
         + jnp.einsum('bnhgqm,bmhd->bnqhgd', p[..., nk:nk + N_META], v_meta))
    return o.reshape(bsz, t_len, n_kv * n_grp, d)


def diff_lambda_init(layer):
    return 0.8 - 0.6 * math.exp(-0.3 * layer)


def even_mixer(h, w_in, conv_w, a_log, dt_bias, gdn_norm_w, lam_q1, lam_k1, lam_q2, lam_k2, diff_norm_w, w_out, lambda_init):
    bsz, l_len, _ = h.shape
    pos = jnp.arange(l_len)
    proj = h @ w_in
    qkv_a, z_a, ab_a, q_b, k_b, v_b = split_cols(
        proj, [2 * GDN_QK + GDN_WIDTH, GDN_WIDTH, 4 * GDN_HEADS, 2 * DIFF_HEADS * DIFF_DQK, 2 * DIFF_HEADS * DIFF_DQK, DIFF_WIDTH])
    qkv_a = jax.nn.silu(dwconv_centred(qkv_a, conv_w))
    q_a, k_a, v_a = split_cols(qkv_a, [GDN_QK, GDN_QK, GDN_WIDTH])
    q_a = l2norm(q_a.reshape(bsz, l_len, GDN_HEADS, GDN_DK)) * (GDN_DK ** -0.5)
    k_a = l2norm(k_a.reshape(bsz, l_len, GDN_HEADS, GDN_DK))
    v_a = v_a.reshape(bsz, l_len, GDN_HEADS, GDN_DV)
    a_raw = ab_a[..., :2 * GDN_HEADS].reshape(bsz, l_len, 2, GDN_HEADS).astype(F32)
    b_raw = ab_a[..., 2 * GDN_HEADS:].reshape(bsz, l_len, 2, GDN_HEADS).astype(F32)
    g = -jnp.exp(a_log.astype(F32)) * jax.nn.softplus(a_raw + dt_bias.astype(F32))
    beta = jax.nn.sigmoid(b_raw)
    qp, kp, vp, gp, bp = lead_pad(q_a), lead_pad(k_a), lead_pad(v_a), lead_pad(g), lead_pad(beta)
    o_fwd = gated_delta_rule_chunked(qp, kp, vp, bp[:, :, 0], gp[:, :, 0])
    o_bwd = rev(gated_delta_rule_chunked(rev(qp), rev(kp), rev(vp), rev(bp[:, :, 1]), rev(gp[:, :, 1])))
    o_a = (o_fwd + o_bwd)[:, N_PAD:]
    o_a = rmsnorm(o_a, gdn_norm_w) * jax.nn.silu(z_a.reshape(bsz, l_len, GDN_HEADS, GDN_DV))
    o_a = o_a.reshape(bsz, l_len, GDN_WIDTH)
    q_b = rope_partial(q_b.reshape(bsz, l_len, 2 * DIFF_HEADS, DIFF_DQK), pos).reshape(bsz, l_len, DIFF_HEADS, 2, DIFF_DQK)
    k_b = rope_partial(k_b.reshape(bsz, l_len, 2 * DIFF_HEADS, DIFF_DQK), pos).reshape(bsz, l_len, DIFF_HEADS, 2, DIFF_DQK)
    v_b = v_b.reshape(bsz, l_len, DIFF_HEADS, DIFF_DV)
    lam = (jnp.exp(jnp.sum(lam_q1.astype(F32) * lam_k1.astype(F32)))
           - jnp.exp(jnp.sum(lam_q2.astype(F32) * lam_k2.astype(F32))) + lambda_init)
    o_b = diff_attention_blocks(lead_pad(q_b), lead_pad(k_b), lead_pad(v_b), lam)[:, N_PAD:]
    o_b = rmsnorm(o_b, diff_norm_w, eps=1e-5) * (1.0 - lambda_init)
    o_b = o_b.reshape(bsz, l_len, DIFF_WIDTH)
    return jnp.concatenate([o_a.astype(h.dtype), o_b.astype(h.dtype)], axis=-1) @ w_out


def odd_mixer(h, w_in, sink, conv_w, conv_b, a_log, dt_bias, d_skip, ssd_norm_w, w_out):
    bsz, l_len, _ = h.shape
    pos = jnp.arange(l_len)
    proj = h @ w_in
    q_c, k_c, v_c, z_d, xbc, dt_raw = split_cols(
        proj, [SWA_WIDTH, SWA_KV_HEADS * HEAD_DIM, SWA_KV_HEADS * HEAD_DIM, SSD_WIDTH, SSD_XBC, 2 * SSD_HEADS])
    q_c = rope_partial(q_c.reshape(bsz, l_len, SWA_HEADS, HEAD_DIM), pos).reshape(bsz, l_len, SWA_KV_HEADS, SWA_GROUP, HEAD_DIM)
    k_c = rope_partial(k_c.reshape(bsz, l_len, SWA_KV_HEADS, HEAD_DIM), pos)
    v_c = v_c.reshape(bsz, l_len, SWA_KV_HEADS, HEAD_DIM)
    o_c = window_attention_blocks(lead_pad(q_c), lead_pad(k_c), lead_pad(v_c), sink)[:, N_PAD:]
    o_c = o_c.reshape(bsz, l_len, SWA_WIDTH)
    xbc = jax.nn.silu(dwconv_centred(xbc, conv_w) + conv_b.astype(xbc.dtype))
    x_d, b_d, c_d = split_cols(xbc, [SSD_WIDTH, SSD_GROUPS * SSD_STATE, SSD_GROUPS * SSD_STATE])
    x_d = x_d.reshape(bsz, l_len, SSD_HEADS, SSD_HEADDIM)
    b_d = b_d.reshape(bsz, l_len, SSD_GROUPS, SSD_STATE)
    c_d = c_d.reshape(bsz, l_len, SSD_GROUPS, SSD_STATE)
    dt = jax.nn.softplus(dt_raw.reshape(bsz, l_len, 2, SSD_HEADS).astype(F32) + dt_bias.astype(F32))
    a = -jnp.exp(a_log.astype(F32))
    xp, bp, cp, dtp = lead_pad(x_d), lead_pad(b_d), lead_pad(c_d), lead_pad(dt)
    y_f = ssd_chunked(xp, dtp[:, :, 0], a[0], bp, cp)
    y_b = rev(ssd_chunked(rev(xp), rev(dtp[:, :, 1]), a[1], rev(bp), rev(cp)))
    y = (y_f + y_b)[:, N_PAD:] + x_d * d_skip[:, None].astype(x_d.dtype)
    y = y.reshape(bsz, l_len, SSD_WIDTH) * jax.nn.silu(z_d)
    y = rmsnorm(y.reshape(bsz, l_len, SSD_GROUPS, SSD_WIDTH // SSD_GROUPS),
                ssd_norm_w.reshape(SSD_GROUPS, SSD_WIDTH // SSD_GROUPS)).reshape(bsz, l_len, SSD_WIDTH)
    return jnp.concatenate([o_c.astype(h.dtype), y.astype(h.dtype)], axis=-1) @ w_out


def swiglu(h, w_gate, w_up, w_down):
    return (jax.nn.silu(h @ w_gate) * (h @ w_up)) @ w_down


def _dt_bias(k, shape):
    dt = jnp.exp(jax.random.uniform(k, shape, minval=math.log(1e-3), maxval=math.log(1e-1)))
    return dt + jnp.log(-jnp.expm1(-dt))


def setup_inputs(seed: int = 0) -> dict:
    key = jax.random.key(seed)
    ks = iter(jax.random.split(key, 32))
    nrm = lambda shape, scale: jax.random.normal(next(ks), shape, F32) * scale
    gain = lambda shape: 1.0 + 0.02 * jax.random.normal(next(ks), shape, F32)
    return {
        'x': nrm((BATCH, SEQ, D_MODEL), 1.0),
        'meta_tokens': nrm((N_META, D_MODEL), 1.0),
        'norm_mix': gain((DEPTH, D_MODEL)),
        'norm_ffn': gain((DEPTH, D_MODEL)),
        'norm_final': gain((D_MODEL,)),
        'even_w_in': nrm((N_EVEN, D_MODEL, EVEN_IN), D_MODEL ** -0.5),
        'even_conv': nrm((N_EVEN, CONV_W, 2 * GDN_QK + GDN_WIDTH), CONV_W ** -0.5),
        'gdn_a_log': jnp.log(jax.random.uniform(next(ks), (N_EVEN, 2, GDN_HEADS), minval=1.0, maxval=16.0)),
        'gdn_dt_bias': _dt_bias(next(ks), (N_EVEN, 2, GDN_HEADS)),
        'gdn_norm': gain((N_EVEN, GDN_DV)),
        'diff_lam_q1': nrm((N_EVEN, DIFF_DQK), 0.1),
        'diff_lam_k1': nrm((N_EVEN, DIFF_DQK), 0.1),
        'diff_lam_q2': nrm((N_EVEN, DIFF_DQK), 0.1),
        'diff_lam_k2': nrm((N_EVEN, DIFF_DQK), 0.1),
        'diff_norm': gain((N_EVEN, DIFF_DV)),
        'even_w_out': nrm((N_EVEN, EVEN_MIX, D_MODEL), EVEN_MIX ** -0.5),
        'odd_w_in': nrm((N_ODD, D_MODEL, ODD_IN), D_MODEL ** -0.5),
        'swa_sink': nrm((N_ODD, SWA_HEADS), 0.5),
        'ssd_conv_w': nrm((N_ODD, CONV_W, SSD_XBC), CONV_W ** -0.5),
        'ssd_conv_b': nrm((N_ODD, SSD_XBC), 0.02),
        'ssd_a_log': jnp.log(jax.random.uniform(next(ks), (N_ODD, 2, SSD_HEADS), minval=1.0, maxval=16.0)),
        'ssd_dt_bias': _dt_bias(next(ks), (N_ODD, 2, SSD_HEADS)),
        'ssd_d': gain((N_ODD, SSD_HEADS)),
        'ssd_norm': gain((N_ODD, SSD_WIDTH)),
        'odd_w_out': nrm((N_ODD, ODD_MIX, D_MODEL), ODD_MIX ** -0.5),
        'ffn_w_gate': nrm((DEPTH, D_MODEL, D_FF), D_MODEL ** -0.5),
        'ffn_w_up': nrm((DEPTH, D_MODEL, D_FF), D_MODEL ** -0.5),
        'ffn_w_down': nrm((DEPTH, D_FF, D_MODEL), D_FF ** -0.5),
    }


def reference(x, meta_tokens, norm_mix, norm_ffn, norm_final, even_w_in, even_conv, gdn_a_log, gdn_dt_bias, gdn_norm,
              diff_lam_q1, diff_lam_k1, diff_lam_q2, diff_lam_k2, diff_norm, even_w_out, odd_w_in, swa_sink,
              ssd_conv_w, ssd_conv_b, ssd_a_log, ssd_dt_bias, ssd_d, ssd_norm, odd_w_out, ffn_w_gate, ffn_w_up, ffn_w_down):
    bsz = x.shape[0]
    meta = jnp.broadcast_to(meta_tokens[None].astype(x.dtype), (bsz, N_META, x.shape[-1]))
    h = jnp.concatenate([meta, x], axis=1)
    for i in range(DEPTH):
        j = i // 2
        hn = rmsnorm(h, norm_mix[i])
        if i % 2 == 0:
            h = h + even_mixer(hn, even_w_in[j], even_conv[j], gdn_a_log[j], gdn_dt_bias[j], gdn_norm[j],
                               diff_lam_q1[j], diff_lam_k1[j], diff_lam_q2[j], diff_lam_k2[j], diff_norm[j],
                               even_w_out[j], diff_lambda_init(i))
        else:
            h = h + odd_mixer(hn, odd_w_in[j], swa_sink[j], ssd_conv_w[j], ssd_conv_b[j], ssd_a_log[j],
                              ssd_dt_bias[j], ssd_d[j], ssd_norm[j], odd_w_out[j])
        h = h + swiglu(rmsnorm(h, norm_ffn[i]), ffn_w_gate[i], ffn_w_up[i], ffn_w_down[i])
    h = rmsnorm(h, norm_final)
    return h[:, N_META:]
```

```python
import functools
import math

import jax
import jax.numpy as jnp
from jax import lax
from jax.experimental import pallas as pl
from jax.experimental.pallas import tpu as pltpu

F32 = jnp.float32
BF16 = jnp.bfloat16
HIGHEST = lax.Precision.HIGHEST

N_META = 16
LEAD = 128
N_PAD = LEAD - N_META
HEAD_DIM = 128
ROT_DIM = HEAD_DIM // 4
ROPE_THETA = 500000.0
CONV_W = 7
EPS = 1e-6

GDN_HEADS = 16
GDN_DK = 128
GDN_WIDTH = 2048
GDN_CHUNK = 64
DIFF_HEADS = 8
DIFF_DQK = 128
DIFF_DV = 256
DIFF_WIDTH = 2048
SWA_HEADS = 16
SWA_KV_HEADS = 4
SWA_GROUP = 4
SWA_WIDTH = 2048
SWA_WINDOW = 128
SWA_BLOCK = 128
SSD_WIDTH = 2048
SSD_HEADDIM = 64
SSD_HEADS = 32
SSD_STATE = 128
SSD_GROUPS = 4
SSD_GROUP_HEADS = SSD_HEADS // SSD_GROUPS
SSD_GROUP_WIDTH = SSD_WIDTH // SSD_GROUPS
SSD_XBC = 3072
SSD_CHUNK = 128

LANE = 128
FF_ALIGN = 1024
NEG = -1e30
VMEM_LIMIT = 56 * 1024 * 1024


def _params(*sem):
    return pltpu.CompilerParams(dimension_semantics=sem, vmem_limit_bytes=VMEM_LIMIT)


def _tile(n, pref, align=LANE):
    if n <= pref:
        return n
    t = (pref // align) * align
    while t >= align:
        if n % t == 0:
            return t
        t -= align
    return n


def _dot(a, b, precision=None):
    return jnp.dot(a, b, preferred_element_type=F32, precision=precision)


def _dot_nt(a, b, precision=None):
    return lax.dot_general(a, b, (((1,), (1,)), ((), ())), preferred_element_type=F32, precision=precision)


def _dot_tn(a, b, precision=None):
    return lax.dot_general(a, b, (((0,), (0,)), ((), ())), preferred_element_type=F32, precision=precision)


def _sigmoid(x):
    return 1.0 / (1.0 + jnp.exp(-x))


def _silu(x):
    return x * _sigmoid(x)


def _softplus(x):
    return jnp.maximum(x, 0.0) + jnp.log1p(jnp.exp(-jnp.abs(x)))


def _rmsnorm_kernel(x_ref, w_ref, o_ref, *, eps):
    x = x_ref[...]
    ms = jnp.mean(x * x, axis=-1, keepdims=True)
    o_ref[...] = (x * lax.rsqrt(ms + eps) * w_ref[...]).astype(o_ref.dtype)


def _rmsnorm(h, w, *, out_dtype, row0=0, rows=None, tr):
    bsz, l_len, d = h.shape
    rows = l_len - row0 if rows is None else rows
    off = row0 // tr
    return pl.pallas_call(
        functools.partial(_rmsnorm_kernel, eps=EPS),
        grid=(bsz, rows // tr),
        in_specs=[pl.BlockSpec((None, tr, d), lambda b, i: (b, i + off, 0)),
                  pl.BlockSpec((1, d), lambda b, i: (0, 0))],
        out_specs=pl.BlockSpec((None, tr, d), lambda b, i: (b, i, 0)),
        out_shape=jax.ShapeDtypeStruct((bsz, rows, d), out_dtype),
        compiler_params=_params("parallel", "parallel"),
        name="rmsnorm",
    )(h, w.reshape(1, d))


def _mm_kernel(*refs, n_pairs, nk, has_res, zero_pad, tm):
    ins = refs[:2 * n_pairs]
    pos = 2 * n_pairs
    res_ref = refs[pos] if has_res else None
    pos += int(has_res)
    o_ref = refs[pos]
    acc_ref = refs[pos + 1] if nk > 1 else None

    part = None
    for p in range(n_pairs):
        d = _dot(ins[2 * p][...], ins[2 * p + 1][...])
        part = d if part is None else part + d

    def finish(acc):
        if has_res:
            acc = acc + res_ref[...]
        if zero_pad:
            row = pl.program_id(1) * tm + lax.broadcasted_iota(jnp.int32, acc.shape, 0)
            acc = jnp.where(row < N_PAD, 0.0, acc)
        o_ref[...] = acc.astype(o_ref.dtype)

    if nk == 1:
        finish(part)
    else:
        k = pl.program_id(3)

        @pl.when(k == 0)
        def _():
            acc_ref[...] = part

        @pl.when(k > 0)
        def _():
            acc_ref[...] += part

        @pl.when(k == nk - 1)
        def _():
            finish(acc_ref[...])


def _matmul(pairs, *, out_dtype, tm, tn, nk=1, res=None, zero_pad=False, name):
    bsz, l_len, kdim = pairs[0][0].shape
    n = pairs[0][1].shape[1]
    tk = kdim // nk
    in_specs, args = [], []
    for a, b in pairs:
        in_specs += [pl.BlockSpec((None, tm, tk), lambda b_, i, j, k: (b_, i, k)),
                     pl.BlockSpec((tk, tn), lambda b_, i, j, k: (k, j))]
        args += [a, b]
    if res is not None:
        in_specs.append(pl.BlockSpec((None, tm, tn), lambda b_, i, j, k: (b_, i, j)))
        args.append(res)
    return pl.pallas_call(
        functools.partial(_mm_kernel, n_pairs=len(pairs), nk=nk, has_res=res is not None,
                          zero_pad=zero_pad, tm=tm),
        grid=(bsz, l_len // tm, n // tn, nk),
        in_specs=in_specs,
        out_specs=pl.BlockSpec((None, tm, tn), lambda b_, i, j, k: (b_, i, j)),
        out_shape=jax.ShapeDtypeStruct((bsz, l_len, n), out_dtype),
        scratch_shapes=[pltpu.VMEM((tm, tn), F32)] if nk > 1 else [],
        compiler_params=_params("parallel", "parallel", "parallel", "arbitrary"),
        name=name,
    )(*args)


def _swiglu_kernel(a_ref, g_ref, u_ref, o_ref):
    a = a_ref[...]
    g = _dot(a, g_ref[...])
    u = _dot(a, u_ref[...])
    o_ref[...] = (_silu(g) * u).astype(o_ref.dtype)


def _swiglu_up(a, w_gate, w_up, *, tm, tn):
    bsz, l_len, kdim = a.shape
    n = w_gate.shape[1]
    return pl.pallas_call(
        _swiglu_kernel,
        grid=(bsz, l_len // tm, n // tn),
        in_specs=[pl.BlockSpec((None, tm, kdim), lambda b, i, j: (b, i, 0)),
                  pl.BlockSpec((kdim, tn), lambda b, i, j: (0, j)),
                  pl.BlockSpec((kdim, tn), lambda b, i, j: (0, j))],
        out_specs=pl.BlockSpec((None, tm, tn), lambda b, i, j: (b, i, j)),
        out_shape=jax.ShapeDtypeStruct((bsz, l_len, n), BF16),
        compiler_params=_params("parallel", "parallel", "parallel"),
        name="swiglu_up",
    )(a, w_gate, w_up)


def _conv_kernel(*refs, l_len, rc, has_bias):
    if has_bias:
        x_ref, w_ref, b_ref, o_ref, xp_ref = refs
    else:
        x_ref, w_ref, o_ref, xp_ref = refs
        b_ref = None
    tc = x_ref.shape[-1]
    halo = 8
    xp_ref[0:halo, :] = jnp.zeros((halo, tc), F32)
    xp_ref[l_len + halo:l_len + 2 * halo, :] = jnp.zeros((halo, tc), F32)
    xp_ref[halo:l_len + halo, :] = x_ref[...]
    w = w_ref[...]
    reach = CONV_W // 2
    for c in range(l_len // rc):
        r0 = c * rc
        acc = None
        for j in range(CONV_W):
            s = r0 + halo - reach + j
            t = xp_ref[s:s + rc, :] * w[j:j + 1, :]
            acc = t if acc is None else acc + t
        if has_bias:
            acc = acc + b_ref[...]
        y = _silu(acc)
        if r0 < N_PAD:
            row = r0 + lax.broadcasted_iota(jnp.int32, y.shape, 0)
            y = jnp.where(row < N_PAD, 0.0, y)
        o_ref[r0:r0 + rc, :] = y


def _conv_silu(proj, col0, width, w, bias, *, tc=256, rc=128):
    bsz, l_len, _ = proj.shape
    off = col0 // tc
    in_specs = [pl.BlockSpec((None, l_len, tc), lambda b, j: (b, 0, j + off)),
                pl.BlockSpec((CONV_W, tc), lambda b, j: (0, j))]
    args = [proj, w]
    if bias is not None:
        in_specs.append(pl.BlockSpec((1, tc), lambda b, j: (0, j)))
        args.append(bias.reshape(1, width))
    return pl.pallas_call(
        functools.partial(_conv_kernel, l_len=l_len, rc=rc, has_bias=bias is not None),
        grid=(bsz, width // tc),
        in_specs=in_specs,
        out_specs=pl.BlockSpec((None, l_len, tc), lambda b, j: (b, 0, j)),
        out_shape=jax.ShapeDtypeStruct((bsz, l_len, width), F32),
        scratch_shapes=[pltpu.VMEM((l_len + 16, tc), F32)],
        compiler_params=_params("parallel", "parallel"),
        name="conv_silu",
    )(*args)


def _rope(x, cf, s1, s2):
    half = ROT_DIM // 2
    return x * cf + pltpu.roll(x, LANE - half, 1) * s1 + pltpu.roll(x, half, 1) * s2


def _rope_cast_kernel(q_ref, k_ref, v_ref, cf_ref, s1_ref, s2_ref, o_ref, *, q_scale):
    cf, s1, s2 = cf_ref[...], s1_ref[...], s2_ref[...]
    wq, wk, wv = q_ref.shape[-1], k_ref.shape[-1], v_ref.shape[-1]
    for hd in range(wq // HEAD_DIM):
        sl = slice(hd * HEAD_DIM, (hd + 1) * HEAD_DIM)
        o_ref[:, sl] = (_rope(q_ref[:, sl], cf, s1, s2) * q_scale).astype(o_ref.dtype)
    for hd in range(wk // HEAD_DIM):
        sl = slice(hd * HEAD_DIM, (hd + 1) * HEAD_DIM)
        o_ref[:, wq + hd * HEAD_DIM:wq + (hd + 1) * HEAD_DIM] = _rope(k_ref[:, sl], cf, s1, s2).astype(o_ref.dtype)
    o_ref[:, wq + wk:wq + wk + wv] = v_ref[...].astype(o_ref.dtype)


def _rope_cast(proj, cols, tables, *, tr):
    bsz, l_len, _ = proj.shape
    width = sum(w for _, w in cols)
    tab_spec = pl.BlockSpec((tr, LANE), lambda b, i: (i, 0))

    def col_spec(col0, w):
        return pl.BlockSpec((None, tr, w), lambda b, i: (b, i, col0 // w))

    return pl.pallas_call(
        functools.partial(_rope_cast_kernel, q_scale=HEAD_DIM ** -0.5),
        grid=(bsz, l_len // tr),
        in_specs=[col_spec(*c) for c in cols] + [tab_spec, tab_spec, tab_spec],
        out_specs=pl.BlockSpec((None, tr, width), lambda b, i: (b, i, 0)),
        out_shape=jax.ShapeDtypeStruct((bsz, l_len, width), BF16),
        compiler_params=_params("parallel", "parallel"),
        name="rope_cast",
    )(proj, proj, proj, *tables)


def _rope_tables(l_pad):
    half = ROT_DIM // 2
    inv = ROPE_THETA ** (-jnp.arange(0, ROT_DIM, 2, dtype=F32) / ROT_DIM)
    pos = (jnp.arange(l_pad) - N_PAD).astype(F32)
    ang = pos[:, None] * inv[None, :]
    cos, sin = jnp.cos(ang), jnp.sin(ang)
    ones = jnp.ones((l_pad, LANE - ROT_DIM), F32)
    zeros = jnp.zeros((l_pad, LANE - ROT_DIM), F32)
    zh = jnp.zeros((l_pad, half), F32)
    cf = jnp.concatenate([cos, cos, ones], axis=1)
    s1 = jnp.concatenate([-sin, zh, zeros], axis=1)
    s2 = jnp.concatenate([zh, sin, zeros], axis=1)
    return cf, s1, s2


def _neumann_inverse(lm, eye):
    x = eye - lm
    p = _dot(lm, lm, HIGHEST)
    n_sq = int(math.log2(lm.shape[0])) - 1
    for t in range(n_sq):
        x = x + _dot(x, p, HIGHEST)
        if t + 1 < n_sq:
            p = _dot(p, p, HIGHEST)
    return x


def _gdn_kernel(*refs, rev, final, nc):
    if final:
        q_ref, k_ref, v_ref, ab_ref, prm_ref, ob_ref, z_ref, nw_ref, o_ref, s_ref = refs
    else:
        q_ref, k_ref, v_ref, ab_ref, prm_ref, o_ref, s_ref = refs
    c = GDN_CHUNK
    n = pl.program_id(1)

    @pl.when(n == 0)
    def _():
        s_ref[...] = jnp.zeros(s_ref.shape, F32)

    chunk = (nc - 1 - n) if rev else n
    rows = chunk * c + lax.broadcasted_iota(jnp.int32, (c, 1), 0)
    valid = rows >= N_PAD
    d = 1 if rev else 0
    nh = GDN_HEADS

    ab = ab_ref[...]
    g_all = -jnp.exp(prm_ref[0:1, :]) * _softplus(ab + prm_ref[1:2, :])
    g_all = jnp.where(valid, g_all, 0.0)
    beta_all = jnp.where(valid, _sigmoid(ab), 0.0)

    ri = lax.broadcasted_iota(jnp.int32, (c, c), 0)
    ci = lax.broadcasted_iota(jnp.int32, (c, c), 1)
    incl = (ri <= ci) if rev else (ri >= ci)
    strict = (ri < ci) if rev else (ri > ci)
    eye = (ri == ci).astype(F32)
    cum = _dot(incl.astype(F32), g_all, HIGHEST)
    cum_t = _dot_tn(cum, eye, HIGHEST)
    tot = cum[0:1, :] if rev else cum[c - 1:c, :]

    for h in range(nh):
        col = d * nh + h
        sl = slice(h * GDN_DK, (h + 1) * GDN_DK)
        gcol = cum[:, col:col + 1]
        grow = cum_t[col:col + 1, :]
        tot_h = tot[:, col:col + 1]
        bcol = beta_all[:, 2 * nh + col:2 * nh + col + 1]
        qh, kh, vh = q_ref[:, sl], k_ref[:, sl], v_ref[:, sl]
        qn = qh * lax.rsqrt(jnp.sum(qh * qh, axis=-1, keepdims=True) + 1e-6) * (GDN_DK ** -0.5)
        kn = kh * lax.rsqrt(jnp.sum(kh * kh, axis=-1, keepdims=True) + 1e-6)
        decay = jnp.where(incl, jnp.exp(jnp.where(incl, gcol - grow, 0.0)), 0.0)
        kb = kn * bcol
        lower = jnp.where(strict, _dot_nt(kb, kn) * decay, 0.0)
        t_inv = _neumann_inverse(lower, eye)
        egc = jnp.exp(gcol)
        u = _dot(t_inv, vh * bcol)
        w = _dot(t_inv, kb * egc)
        qk = _dot_nt(qn, kn) * decay
        s = s_ref[h]
        v_new = u - _dot(w, s)
        o = _dot(qn * egc, s) + _dot(qk, v_new)
        s_ref[h] = s * jnp.exp(tot_h) + _dot_tn(kn * jnp.exp(tot_h - gcol), v_new)
        if final:
            o = o + ob_ref[:, sl]
            o = o * lax.rsqrt(jnp.mean(o * o, axis=-1, keepdims=True) + EPS) * nw_ref[...]
            o = o * _silu(z_ref[:, sl])
        o_ref[:, sl] = o.astype(o_ref.dtype)


def _gdn_pass(qkv, ab, prm, *, rev, extra=None):
    bsz, l_len, _ = qkv.shape
    c = GDN_CHUNK
    nc = l_len // c
    wd = GDN_WIDTH
    ch = (lambda n: nc - 1 - n) if rev else (lambda n: n)
    in_specs = [pl.BlockSpec((None, c, wd), lambda b, n: (b, ch(n), 0)),
                pl.BlockSpec((None, c, wd), lambda b, n: (b, ch(n), 1)),
                pl.BlockSpec((None, c, wd), lambda b, n: (b, ch(n), 2)),
                pl.BlockSpec((None, c, LANE), lambda b, n: (b, ch(n), 0)),
                pl.BlockSpec((8, LANE), lambda b, n: (0, 0))]
    args = [qkv, qkv, qkv, ab, prm]
    final = extra is not None
    if final:
        o_other, proj, z_col0, norm_w = extra
        zoff = z_col0 // wd
        in_specs += [pl.BlockSpec((None, c, wd), lambda b, n: (b, ch(n), 0)),
                     pl.BlockSpec((None, c, wd), lambda b, n: (b, ch(n), zoff)),
                     pl.BlockSpec((1, GDN_DK), lambda b, n: (0, 0))]
        args += [o_other, proj, norm_w.reshape(1, GDN_DK)]
    return pl.pallas_call(
        functools.partial(_gdn_kernel, rev=rev, final=final, nc=nc),
        grid=(bsz, nc),
        in_specs=in_specs,
        out_specs=pl.BlockSpec((None, c, wd), lambda b, n: (b, ch(n), 0)),
        out_shape=jax.ShapeDtypeStruct((bsz, l_len, wd), BF16 if final else F32),
        scratch_shapes=[pltpu.VMEM((GDN_HEADS, GDN_DK, GDN_DK), F32)],
        compiler_params=_params("parallel", "arbitrary"),
        name="gdn_bwd" if rev else "gdn_fwd",
    )(*args)


def _diff_kernel(q_ref, k_ref, v_ref, lam_ref, nw_ref, o_ref, *, lam_init):
    l_len = k_ref.shape[0]
    key_ok = lax.broadcasted_iota(jnp.int32, (1, l_len), 1) >= N_PAD
    probs = []
    for m in range(2):
        sl = slice(m * DIFF_DQK, (m + 1) * DIFF_DQK)
        s = _dot_nt(q_ref[:, sl], k_ref[:, sl])
        s = jnp.where(key_ok, s, NEG)
        e = jnp.exp(s - jnp.max(s, axis=-1, keepdims=True))
        probs.append(e * (1.0 / jnp.sum(e, axis=-1, keepdims=True)))
    lv = lam_ref[...]
    lam = (jnp.exp(jnp.sum(lv[0:1] * lv[1:2], axis=-1, keepdims=True))
           - jnp.exp(jnp.sum(lv[2:3] * lv[3:4], axis=-1, keepdims=True)) + lam_init)
    wts = probs[0] - lam * probs[1]
    o = _dot(wts.astype(BF16), v_ref[...])
    o = o * lax.rsqrt(jnp.mean(o * o, axis=-1, keepdims=True) + 1e-5) * nw_ref[...]
    o_ref[...] = (o * (1.0 - lam_init)).astype(o_ref.dtype)


def _diff_attention(qkv, lam_vecs, norm_w, lam_init, *, tq):
    bsz, l_len, _ = qkv.shape
    wq = 2 * DIFF_DQK
    nhd = DIFF_HEADS
    return pl.pallas_call(
        functools.partial(_diff_kernel, lam_init=lam_init),
        grid=(bsz, nhd, l_len // tq),
        in_specs=[pl.BlockSpec((None, tq, wq), lambda b, h, i: (b, i, h)),
                  pl.BlockSpec((None, l_len, wq), lambda b, h, i: (b, 0, nhd + h)),
                  pl.BlockSpec((None, l_len, DIFF_DV), lambda b, h, i: (b, 0, 2 * nhd + h)),
                  pl.BlockSpec((4, DIFF_DQK), lambda b, h, i: (0, 0)),
                  pl.BlockSpec((1, DIFF_DV), lambda b, h, i: (0, 0))],
        out_specs=pl.BlockSpec((None, tq, DIFF_DV), lambda b, h, i: (b, i, h)),
        out_shape=jax.ShapeDtypeStruct((bsz, l_len, DIFF_WIDTH), BF16),
        compiler_params=_params("parallel", "parallel", "parallel"),
        name="diff_attention",
    )(qkv, qkv, qkv, lam_vecs, norm_w.reshape(1, DIFF_DV))


def _swa_kernel(sink_ref, q_ref, k_ref, v_ref, o_ref):
    l_len = k_ref.shape[0]
    blk = SWA_BLOCK
    band = 3 * blk
    kvh = pl.program_id(1)
    n = pl.program_id(2)
    start = pl.multiple_of(jnp.clip((n - 1) * blk, 0, l_len - band), blk)
    kb = k_ref[pl.ds(start, band), :]
    vb = v_ref[pl.ds(start, band), :]
    km = k_ref[0:LEAD, :]
    vm = v_ref[0:LEAD, :]
    q = jnp.concatenate([q_ref[:, g * HEAD_DIM:(g + 1) * HEAD_DIM] for g in range(SWA_GROUP)], axis=0)
    rows = SWA_GROUP * blk
    sb = _dot_nt(q, kb)
    sm = _dot_nt(q, km)
    qpos = n * blk + (lax.broadcasted_iota(jnp.int32, (rows, band), 0) & (blk - 1))
    kpos = start + lax.broadcasted_iota(jnp.int32, (rows, band), 1)
    band_ok = (jnp.abs(qpos - kpos) <= SWA_WINDOW) & (kpos >= LEAD)
    sb = jnp.where(band_ok, sb, NEG)
    sm = jnp.where(lax.broadcasted_iota(jnp.int32, (rows, LEAD), 1) >= N_PAD, sm, NEG)
    sk = jnp.concatenate([jnp.full((blk, 1), sink_ref[kvh * SWA_GROUP + g], F32) for g in range(SWA_GROUP)],
                         axis=0)
    mx = jnp.maximum(jnp.maximum(jnp.max(sb, axis=-1, keepdims=True), jnp.max(sm, axis=-1, keepdims=True)), sk)
    eb = jnp.exp(sb - mx)
    em = jnp.exp(sm - mx)
    den = jnp.sum(eb, axis=-1, keepdims=True) + jnp.sum(em, axis=-1, keepdims=True) + jnp.exp(sk - mx)
    o = (_dot(eb.astype(BF16), vb) + _dot(em.astype(BF16), vm)) * (1.0 / den)
    for g in range(SWA_GROUP):
        o_ref[:, g * HEAD_DIM:(g + 1) * HEAD_DIM] = o[g * blk:(g + 1) * blk].astype(o_ref.dtype)


def _window_attention(qkv, sink):
    bsz, l_len, _ = qkv.shape
    wq = SWA_GROUP * HEAD_DIM
    koff = SWA_WIDTH // HEAD_DIM
    voff = koff + SWA_KV_HEADS
    return pl.pallas_call(
        _swa_kernel,
        grid=(bsz, SWA_KV_HEADS, l_len // SWA_BLOCK),
        in_specs=[pl.BlockSpec(memory_space=pltpu.SMEM),
                  pl.BlockSpec((None, SWA_BLOCK, wq), lambda b, h, i: (b, i, h)),
                  pl.BlockSpec((None, l_len, HEAD_DIM), lambda b, h, i: (b, 0, koff + h)),
                  pl.BlockSpec((None, l_len, HEAD_DIM), lambda b, h, i: (b, 0, voff + h))],
        out_specs=pl.BlockSpec((None, SWA_BLOCK, wq), lambda b, h, i: (b, i, h)),
        out_shape=jax.ShapeDtypeStruct((bsz, l_len, SWA_WIDTH), BF16),
        compiler_params=_params("parallel", "parallel", "parallel"),
        name="window_attention",
    )(sink, qkv, qkv, qkv)


def _ssd_kernel(*refs, rev, final, nc):
    if final:
        x_ref, b_ref, c_ref, dt_ref, prm_ref, yb_ref, z_ref, drow_ref, nw_ref, o_ref, s_ref = refs
    else:
        x_ref, b_ref, c_ref, dt_ref, prm_ref, o_ref, s_ref = refs
    q = SSD_CHUNK
    n = pl.program_id(1)

    @pl.when(n == 0)
    def _():
        s_ref[...] = jnp.zeros(s_ref.shape, F32)

    chunk = (nc - 1 - n) if rev else n
    rows = chunk * q + lax.broadcasted_iota(jnp.int32, (q, 1), 0)
    valid = rows >= N_PAD
    d = 1 if rev else 0
    p = SSD_HEADDIM
    gh = SSD_GROUP_HEADS
    gw = SSD_GROUP_WIDTH

    dt = jnp.where(valid, _softplus(dt_ref[...] + prm_ref[1:2, :]), 0.0)
    da = dt * (-jnp.exp(prm_ref[0:1, :]))
    ri = lax.broadcasted_iota(jnp.int32, (q, q), 0)
    ci = lax.broadcasted_iota(jnp.int32, (q, q), 1)
    incl = (ri <= ci) if rev else (ri >= ci)
    eye = (ri == ci).astype(F32)
    cum = _dot(incl.astype(F32), da, HIGHEST)
    cum_t = _dot_tn(cum, eye, HIGHEST)
    tot = cum[0:1, :] if rev else cum[q - 1:q, :]
    ecum = jnp.exp(cum)
    dstate = jnp.exp(tot - cum)
    etot = jnp.exp(tot)

    ys = []
    for gi in range(SSD_GROUPS):
        bg = b_ref[:, gi * SSD_STATE:(gi + 1) * SSD_STATE]
        cg = c_ref[:, gi * SSD_STATE:(gi + 1) * SSD_STATE]
        cb = _dot_nt(cg, bg)
        sg = s_ref[gi * gw:(gi + 1) * gw, :]
        y_off = _dot_nt(cg, sg)
        xs, decs = [], []
        for hh in range(gh):
            h = gi * gh + hh
            col = d * SSD_HEADS + h
            cumc = cum[:, col:col + 1]
            cumr = cum_t[col:col + 1, :]
            seg = jnp.where(incl, jnp.exp(jnp.where(incl, cumc - cumr, 0.0)), 0.0)
            xdt = x_ref[:, h * p:(h + 1) * p] * dt[:, col:col + 1]
            y_h = _dot(cb * seg, xdt) + y_off[:, hh * p:(hh + 1) * p] * ecum[:, col:col + 1]
            ys.append(y_h)
            xs.append(xdt * dstate[:, col:col + 1])
            decs.append(jnp.broadcast_to(etot[:, col:col + 1], (p, 1)))
        st = _dot_tn(jnp.concatenate(xs, axis=1), bg)
        s_ref[gi * gw:(gi + 1) * gw, :] = sg * jnp.concatenate(decs, axis=0) + st
    y = jnp.concatenate(ys, axis=1)
    if final:
        y = y + yb_ref[...] + x_ref[...] * drow_ref[...]
        y = y * _silu(z_ref[...])
        for gi in range(SSD_GROUPS):
            sl = slice(gi * gw, (gi + 1) * gw)
            yg = y[:, sl]
            yg = yg * lax.rsqrt(jnp.mean(yg * yg, axis=-1, keepdims=True) + EPS) * nw_ref[:, sl]
            o_ref[:, sl] = yg.astype(o_ref.dtype)
    else:
        o_ref[...] = y


def _ssd_pass(xbc, dt_raw, prm, *, rev, extra=None):
    bsz, l_len, _ = xbc.shape
    q = SSD_CHUNK
    nc = l_len // q
    wd = SSD_WIDTH
    gs = SSD_GROUPS * SSD_STATE
    ch = (lambda n: nc - 1 - n) if rev else (lambda n: n)
    in_specs = [pl.BlockSpec((None, q, wd), lambda b, n: (b, ch(n), 0)),
                pl.BlockSpec((None, q, gs), lambda b, n: (b, ch(n), wd // gs)),
                pl.BlockSpec((None, q, gs), lambda b, n: (b, ch(n), wd // gs + 1)),
                pl.BlockSpec((None, q, LANE), lambda b, n: (b, ch(n), 0)),
                pl.BlockSpec((8, LANE), lambda b, n: (0, 0))]
    args = [xbc, xbc, xbc, dt_raw, prm]
    final = extra is not None
    if final:
        y_other, proj, z_col0, d_row, norm_w = extra
        zoff = z_col0 // wd
        in_specs += [pl.BlockSpec((None, q, wd), lambda b, n: (b, ch(n), 0)),
                     pl.BlockSpec((None, q, wd), lambda b, n: (b, ch(n), zoff)),
                     pl.BlockSpec((1, wd), lambda b, n: (0, 0)),
                     pl.BlockSpec((1, wd), lambda b, n: (0, 0))]
        args += [y_other, proj, d_row, norm_w.reshape(1, wd)]
    return pl.pallas_call(
        functools.partial(_ssd_kernel, rev=rev, final=final, nc=nc),
        grid=(bsz, nc),
        in_specs=in_specs,
        out_specs=pl.BlockSpec((None, q, wd), lambda b, n: (b, ch(n), 0)),
        out_shape=jax.ShapeDtypeStruct((bsz, l_len, wd), BF16 if final else F32),
        scratch_shapes=[pltpu.VMEM((wd, SSD_STATE), F32)],
        compiler_params=_params("parallel", "arbitrary"),
        name="ssd_bwd" if rev else "ssd_fwd",
    )(*args)


def _lane_params(*rows):
    out = jnp.zeros((8, LANE), F32)
    for i, r in enumerate(rows):
        r = r.astype(F32).reshape(-1)
        out = out.at[i, :r.shape[0]].set(r)
    return out


def _pad_cols(w, n):
    return jnp.pad(w, ((0, 0), (0, n - w.shape[1])))


def _diff_lambda_init(layer):
    return 0.8 - 0.6 * math.exp(-0.3 * layer)


def _tiles(l_pad):
    tm = _tile(l_pad, 1056, 16)
    return tm


def _even_mixer(h, hn, w_in, conv_w, a_log, dt_bias, gdn_norm_w, lam_vecs, diff_norm_w, w_out, lam_init, tables):
    bsz, l_pad, d = h.shape
    tm = _tiles(l_pad)
    qkv_w = 3 * GDN_WIDTH
    main_w = qkv_w + GDN_WIDTH
    ab_w = 4 * GDN_HEADS
    w_main = jnp.concatenate([w_in[:, :main_w], w_in[:, main_w + ab_w:]], axis=1).astype(BF16)
    w_ab = _pad_cols(w_in[:, main_w:main_w + ab_w], LANE).astype(BF16)
    proj = _matmul([(hn, w_main)], out_dtype=F32, tm=tm, tn=_tile(w_main.shape[1], 1024), name="even_in_proj")
    ab = _matmul([(hn, w_ab)], out_dtype=F32, tm=tm, tn=LANE, name="even_ab_proj")

    qkv_a = _conv_silu(proj, 0, qkv_w, conv_w, None)
    prm = _lane_params(a_log, dt_bias)
    o_bwd = _gdn_pass(qkv_a, ab, prm, rev=True)
    o_a = _gdn_pass(qkv_a, ab, prm, rev=False, extra=(o_bwd, proj, qkv_w, gdn_norm_w))

    qk_w = 2 * DIFF_HEADS * DIFF_DQK
    cols = ((main_w, qk_w), (main_w + qk_w, qk_w), (main_w + 2 * qk_w, DIFF_WIDTH))
    qkv_b = _rope_cast(proj, cols, tables, tr=_tile(l_pad, 384, 16))
    o_b = _diff_attention(qkv_b, lam_vecs, diff_norm_w, lam_init, tq=_tile(l_pad, 384, 16))

    w_o = w_out.astype(BF16)
    return _matmul([(o_a, w_o[:GDN_WIDTH]), (o_b, w_o[GDN_WIDTH:])], out_dtype=F32, tm=tm,
                   tn=_tile(d, 512), res=h, zero_pad=True, name="even_out_proj")


def _odd_mixer(h, hn, w_in, sink, conv_w, conv_b, a_log, dt_bias, d_skip, ssd_norm_w, w_out, tables):
    bsz, l_pad, d = h.shape
    tm = _tiles(l_pad)
    kv_w = SWA_KV_HEADS * HEAD_DIM
    swa_w = SWA_WIDTH + 2 * kv_w
    main_w = swa_w + SSD_WIDTH + SSD_XBC
    w_main = jnp.concatenate([w_in[:, :SWA_WIDTH], w_in[:, swa_w:swa_w + SSD_WIDTH], w_in[:, SWA_WIDTH:swa_w],
                              w_in[:, swa_w + SSD_WIDTH:main_w]], axis=1).astype(BF16)
    w_dt = _pad_cols(w_in[:, main_w:], LANE).astype(BF16)
    proj = _matmul([(hn, w_main)], out_dtype=F32, tm=tm, tn=_tile(main_w, 1024), name="odd_in_proj")
    dt_raw = _matmul([(hn, w_dt)], out_dtype=F32, tm=tm, tn=LANE, name="odd_dt_proj")

    z_col = SWA_WIDTH
    k_col = z_col + SSD_WIDTH
    cols = ((0, SWA_WIDTH), (k_col, kv_w), (k_col + kv_w, kv_w))
    qkv_c = _rope_cast(proj, cols, tables, tr=_tile(l_pad, 384, 16))
    o_c = _window_attention(qkv_c, sink.astype(F32))

    xbc = _conv_silu(proj, k_col + 2 * kv_w, SSD_XBC, conv_w, conv_b)
    prm = _lane_params(a_log, dt_bias)
    d_row = jnp.repeat(d_skip.astype(F32), SSD_HEADDIM).reshape(1, SSD_WIDTH)
    y_bwd = _ssd_pass(xbc, dt_raw, prm, rev=True)
    y = _ssd_pass(xbc, dt_raw, prm, rev=False, extra=(y_bwd, proj, z_col, d_row, ssd_norm_w))

    w_o = w_out.astype(BF16)
    return _matmul([(o_c, w_o[:SWA_WIDTH]), (y, w_o[SWA_WIDTH:])], out_dtype=F32, tm=tm,
                   tn=_tile(d, 512), res=h, zero_pad=True, name="odd_out_proj")


def _ffn(h, norm_w, w_gate, w_up, w_down):
    bsz, l_pad, d = h.shape
    tm = _tiles(l_pad)
    f = w_gate.shape[1]
    fp = -(-f // FF_ALIGN) * FF_ALIGN
    wg = _pad_cols(w_gate, fp).astype(BF16)
    wu = _pad_cols(w_up, fp).astype(BF16)
    wd = jnp.pad(w_down, ((0, fp - f), (0, 0))).astype(BF16)
    hn = _rmsnorm(h, norm_w, out_dtype=BF16, tr=_tile(l_pad, 352, 16))
    act = _swiglu_up(hn, wg, wu, tm=tm, tn=_tile(fp, 512))
    nk = 2 if fp > 4096 else 1
    return _matmul([(act, wd)], out_dtype=F32, tm=tm, tn=_tile(d, 512), nk=nk, res=h, name="ffn_down")


def kernel(x, meta_tokens, norm_mix, norm_ffn, norm_final, even_w_in, even_conv, gdn_a_log, gdn_dt_bias, gdn_norm, diff_lam_q1, diff_lam_k1, diff_lam_q2, diff_lam_k2, diff_norm, even_w_out, odd_w_in, swa_sink, ssd_conv_w, ssd_conv_b, ssd_a_log, ssd_dt_bias, ssd_d, ssd_norm, odd_w_out, ffn_w_gate, ffn_w_up, ffn_w_down):
    bsz, seq, d = x.shape
    depth = norm_mix.shape[0]
    l_pad = LEAD + seq
    meta = jnp.broadcast_to(meta_tokens[None].astype(x.dtype), (bsz, N_META, d))
    h = jnp.concatenate([jnp.zeros((bsz, N_PAD, d), x.dtype), meta, x], axis=1)
    tables = _rope_tables(l_pad)
    tr = _tile(l_pad, 352, 16)
    for i in range(depth):
        j = i // 2
        hn = _rmsnorm(h, norm_mix[i], out_dtype=BF16, tr=tr)
        if i % 2 == 0:
            lam_vecs = jnp.stack([diff_lam_q1[j], diff_lam_k1[j], diff_lam_q2[j], diff_lam_k2[j]]).astype(F32)
            h = _even_mixer(h, hn, even_w_in[j], even_conv[j], gdn_a_log[j], gdn_dt_bias[j], gdn_norm[j],
                            lam_vecs, diff_norm[j], even_w_out[j], _diff_lambda_init(i), tables)
        else:
            h = _odd_mixer(h, hn, odd_w_in[j], swa_sink[j], ssd_conv_w[j], ssd_conv_b[j], ssd_a_log[j],
                           ssd_dt_bias[j], ssd_d[j], ssd_norm[j], odd_w_out[j], tables)
        h = _ffn(h, norm_ffn[i], ffn_w_gate[i], ffn_w_up[i], ffn_w_down[i])
    return _rmsnorm(h, norm_final, out_dtype=x.dtype, row0=LEAD, rows=seq, tr=LANE)
```

```python
import functools
import math

import jax
import jax.numpy as jnp
from jax import lax
from jax.experimental import pallas as pl
from jax.experimental.pallas import tpu as pltpu

F32 = jnp.float32
BF16 = jnp.bfloat16
HIGHEST = lax.Precision.HIGHEST

N_META = 16
LEAD = 128
N_PAD = LEAD - N_META
HEAD_DIM = 128
ROT_DIM = HEAD_DIM // 4
ROPE_THETA = 500000.0
CONV_W = 7
EPS = 1e-6

GDN_HEADS = 16
GDN_DK = 128
GDN_WIDTH = 2048
GDN_CHUNK = 64
DIFF_HEADS = 8
DIFF_DQK = 128
DIFF_DV = 256
DIFF_WIDTH = 2048
SWA_HEADS = 16
SWA_KV_HEADS = 4
SWA_GROUP = 4
SWA_WIDTH = 2048
SWA_WINDOW = 128
SWA_BLOCK = 128
SSD_WIDTH = 2048
SSD_HEADDIM = 64
SSD_HEADS = 32
SSD_STATE = 128
SSD_GROUPS = 4
SSD_GROUP_HEADS = SSD_HEADS // SSD_GROUPS
SSD_GROUP_WIDTH = SSD_WIDTH // SSD_GROUPS
SSD_XBC = 3072
SSD_CHUNK = 128

LANE = 128
NEG = -1e30
VMEM_LIMIT = 56 * 1024 * 1024


def _params(*sem):
    return pltpu.CompilerParams(dimension_semantics=sem, vmem_limit_bytes=VMEM_LIMIT)


def _tile(n, pref, align=LANE):
    if n <= pref:
        return n
    t = (pref // align) * align
    while t >= align:
        if n % t == 0:
            return t
        t -= align
    return n


def _dot(a, b, precision=None):
    return jnp.dot(a, b, preferred_element_type=F32, precision=precision)


def _dot_nt(a, b, precision=None):
    return lax.dot_general(a, b, (((1,), (1,)), ((), ())), preferred_element_type=F32, precision=precision)


def _dot_tn(a, b, precision=None):
    return lax.dot_general(a, b, (((0,), (0,)), ((), ())), preferred_element_type=F32, precision=precision)


def _sigmoid(x):
    return 1.0 / (1.0 + jnp.exp(-x))


def _silu(x):
    return x * _sigmoid(x)


def _softplus(x):
    return jnp.maximum(x, 0.0) + jnp.log1p(jnp.exp(-jnp.abs(x)))


def _rmsnorm_kernel(x_ref, w_ref, o_ref, *, eps):
    x = x_ref[...]
    ms = jnp.mean(x * x, axis=-1, keepdims=True)
    o_ref[...] = (x * lax.rsqrt(ms + eps) * w_ref[...]).astype(o_ref.dtype)


def _rmsnorm(h, w, *, out_dtype, row0=0, rows=None, tr):
    bsz, l_len, d = h.shape
    rows = l_len - row0 if rows is None else rows
    off = row0 // tr
    return pl.pallas_call(
        functools.partial(_rmsnorm_kernel, eps=EPS),
        grid=(bsz, rows // tr),
        in_specs=[pl.BlockSpec((None, tr, d), lambda b, i: (b, i + off, 0)),
                  pl.BlockSpec((1, d), lambda b, i: (0, 0))],
        out_specs=pl.BlockSpec((None, tr, d), lambda b, i: (b, i, 0)),
        out_shape=jax.ShapeDtypeStruct((bsz, rows, d), out_dtype),
        compiler_params=_params("parallel", "parallel"),
        name="rmsnorm",
    )(h, w.reshape(1, d))


def _mm_kernel(*refs, n_pairs, nk, has_res, zero_pad, tm):
    ins = refs[:2 * n_pairs]
    pos = 2 * n_pairs
    res_ref = refs[pos] if has_res else None
    pos += int(has_res)
    o_ref = refs[pos]
    acc_ref = refs[pos + 1] if nk > 1 else None

    part = None
    for p in range(n_pairs):
        d = _dot(ins[2 * p][...], ins[2 * p + 1][...].astype(BF16))
        part = d if part is None else part + d

    def finish(acc):
        if has_res:
            acc = acc + res_ref[...]
        if zero_pad:
            row = pl.program_id(1) * tm + lax.broadcasted_iota(jnp.int32, acc.shape, 0)
            acc = jnp.where(row < N_PAD, 0.0, acc)
        o_ref[...] = acc.astype(o_ref.dtype)

    if nk == 1:
        finish(part)
    else:
        k = pl.program_id(3)

        @pl.when(k == 0)
        def _():
            acc_ref[...] = part

        @pl.when(k > 0)
        def _():
            acc_ref[...] += part

        @pl.when(k == nk - 1)
        def _():
            finish(acc_ref[...])


def _matmul(pairs, *, out_dtype, tm, tn, n=None, nk=1, res=None, zero_pad=False, name):
    bsz, l_len, kdim = pairs[0][0].shape
    n = pairs[0][1].shape[1] if n is None else n
    tk = kdim // nk
    in_specs, args = [], []
    for a, b, row0, col0 in pairs:
        assert row0 % tk == 0 and col0 % tn == 0
        in_specs += [pl.BlockSpec((None, tm, tk), lambda b_, i, j, k: (b_, i, k)),
                     pl.BlockSpec((tk, tn), lambda b_, i, j, k, r=row0 // tk, c=col0 // tn: (k + r, j + c))]
        args += [a, b]
    if res is not None:
        in_specs.append(pl.BlockSpec((None, tm, tn), lambda b_, i, j, k: (b_, i, j)))
        args.append(res)
    return pl.pallas_call(
        functools.partial(_mm_kernel, n_pairs=len(pairs), nk=nk, has_res=res is not None,
                          zero_pad=zero_pad, tm=tm),
        grid=(bsz, l_len // tm, n // tn, nk),
        in_specs=in_specs,
        out_specs=pl.BlockSpec((None, tm, tn), lambda b_, i, j, k: (b_, i, j)),
        out_shape=jax.ShapeDtypeStruct((bsz, l_len, n), out_dtype),
        scratch_shapes=[pltpu.VMEM((tm, tn), F32)] if nk > 1 else [],
        compiler_params=_params("parallel", "parallel", "parallel", "arbitrary"),
        name=name,
    )(*args)


def _swiglu_kernel(a_ref, g_ref, u_ref, o_ref):
    a = a_ref[...]
    g = _dot(a, g_ref[...].astype(BF16))
    u = _dot(a, u_ref[...].astype(BF16))
    o_ref[...] = (_silu(g) * u).astype(o_ref.dtype)


def _swiglu_up(a, w_gate, w_up, *, tm, tn):
    bsz, l_len, kdim = a.shape
    n = w_gate.shape[1]
    return pl.pallas_call(
        _swiglu_kernel,
        grid=(bsz, l_len // tm, n // tn),
        in_specs=[pl.BlockSpec((None, tm, kdim), lambda b, i, j: (b, i, 0)),
                  pl.BlockSpec((kdim, tn), lambda b, i, j: (0, j)),
                  pl.BlockSpec((kdim, tn), lambda b, i, j: (0, j))],
        out_specs=pl.BlockSpec((None, tm, tn), lambda b, i, j: (b, i, j)),
        out_shape=jax.ShapeDtypeStruct((bsz, l_len, n), BF16),
        compiler_params=_params("parallel", "parallel", "parallel"),
        name="swiglu_up",
    )(a, w_gate, w_up)


def _conv_kernel(*refs, l_len, rc, has_bias):
    if has_bias:
        x_ref, w_ref, b_ref, o_ref, xp_ref = refs
    else:
        x_ref, w_ref, o_ref, xp_ref = refs
        b_ref = None
    tc = x_ref.shape[-1]
    halo = 8
    xp_ref[0:halo, :] = jnp.zeros((halo, tc), F32)
    xp_ref[l_len + halo:l_len + 2 * halo, :] = jnp.zeros((halo, tc), F32)
    xp_ref[halo:l_len + halo, :] = x_ref[...]
    w = w_ref[...]
    reach = CONV_W // 2
    for c in range(l_len // rc):
        r0 = c * rc
        acc = None
        for j in range(CONV_W):
            s = r0 + halo - reach + j
            t = xp_ref[s:s + rc, :] * w[j:j + 1, :]
            acc = t if acc is None else acc + t
        if has_bias:
            acc = acc + b_ref[...]
        y = _silu(acc)
        if r0 < N_PAD:
            row = r0 + lax.broadcasted_iota(jnp.int32, y.shape, 0)
            y = jnp.where(row < N_PAD, 0.0, y)
        o_ref[r0:r0 + rc, :] = y


def _conv_silu(proj, col0, width, w, bias, *, tc=256, rc=128):
    bsz, l_len, _ = proj.shape
    off = col0 // tc
    in_specs = [pl.BlockSpec((None, l_len, tc), lambda b, j: (b, 0, j + off)),
                pl.BlockSpec((CONV_W, tc), lambda b, j: (0, j))]
    args = [proj, w]
    if bias is not None:
        in_specs.append(pl.BlockSpec((1, tc), lambda b, j: (0, j)))
        args.append(bias.reshape(1, width))
    return pl.pallas_call(
        functools.partial(_conv_kernel, l_len=l_len, rc=rc, has_bias=bias is not None),
        grid=(bsz, width // tc),
        in_specs=in_specs,
        out_specs=pl.BlockSpec((None, l_len, tc), lambda b, j: (b, 0, j)),
        out_shape=jax.ShapeDtypeStruct((bsz, l_len, width), F32),
        scratch_shapes=[pltpu.VMEM((l_len + 16, tc), F32)],
        compiler_params=_params("parallel", "parallel"),
        name="conv_silu",
    )(*args)


def _rope(x, cf, s1, s2):
    half = ROT_DIM // 2
    return x * cf + pltpu.roll(x, LANE - half, 1) * s1 + pltpu.roll(x, half, 1) * s2


def _rope_cast_kernel(q_ref, k_ref, v_ref, cf_ref, s1_ref, s2_ref, o_ref, *, q_scale):
    cf, s1, s2 = cf_ref[...], s1_ref[...], s2_ref[...]
    wq, wk, wv = q_ref.shape[-1], k_ref.shape[-1], v_ref.shape[-1]
    for hd in range(wq // HEAD_DIM):
        sl = slice(hd * HEAD_DIM, (hd + 1) * HEAD_DIM)
        o_ref[:, sl] = (_rope(q_ref[:, sl], cf, s1, s2) * q_scale).astype(o_ref.dtype)
    for hd in range(wk // HEAD_DIM):
        sl = slice(hd * HEAD_DIM, (hd + 1) * HEAD_DIM)
        o_ref[:, wq + hd * HEAD_DIM:wq + (hd + 1) * HEAD_DIM] = _rope(k_ref[:, sl], cf, s1, s2).astype(o_ref.dtype)
    o_ref[:, wq + wk:wq + wk + wv] = v_ref[...].astype(o_ref.dtype)


def _rope_cast(proj, cols, tables, *, tr):
    bsz, l_len, _ = proj.shape
    width = sum(w for _, w in cols)
    tab_spec = pl.BlockSpec((tr, LANE), lambda b, i: (i, 0))

    def col_spec(col0, w):
        return pl.BlockSpec((None, tr, w), lambda b, i: (b, i, col0 // w))

    return pl.pallas_call(
        functools.partial(_rope_cast_kernel, q_scale=HEAD_DIM ** -0.5),
        grid=(bsz, l_len // tr),
        in_specs=[col_spec(*c) for c in cols] + [tab_spec, tab_spec, tab_spec],
        out_specs=pl.BlockSpec((None, tr, width), lambda b, i: (b, i, 0)),
        out_shape=jax.ShapeDtypeStruct((bsz, l_len, width), BF16),
        compiler_params=_params("parallel", "parallel"),
        name="rope_cast",
    )(proj, proj, proj, *tables)


def _rope_tables(l_pad):
    half = ROT_DIM // 2
    inv = ROPE_THETA ** (-jnp.arange(0, ROT_DIM, 2, dtype=F32) / ROT_DIM)
    pos = (jnp.arange(l_pad) - N_PAD).astype(F32)
    ang = pos[:, None] * inv[None, :]
    cos, sin = jnp.cos(ang), jnp.sin(ang)
    ones = jnp.ones((l_pad, LANE - ROT_DIM), F32)
    zeros = jnp.zeros((l_pad, LANE - ROT_DIM), F32)
    zh = jnp.zeros((l_pad, half), F32)
    cf = jnp.concatenate([cos, cos, ones], axis=1)
    s1 = jnp.concatenate([-sin, zh, zeros], axis=1)
    s2 = jnp.concatenate([zh, sin, zeros], axis=1)
    return cf, s1, s2


def _bdot(a, b):
    return _dot(a.astype(BF16), b.astype(BF16))


def _bdot_nt(a, b):
    return _dot_nt(a.astype(BF16), b.astype(BF16))


def _bdot_tn(a, b):
    return _dot_tn(a.astype(BF16), b.astype(BF16))


def _gdn_kernel(*refs, rev, final, nc):
    if final:
        q_ref, k_ref, v_ref, ab_ref, prm_ref, ob_ref, z_ref, nw_ref, o_ref, s_ref = refs
    else:
        q_ref, k_ref, v_ref, ab_ref, prm_ref, o_ref, s_ref = refs
    c = GDN_CHUNK
    n = pl.program_id(1)

    @pl.when(n == 0)
    def _():
        s_ref[...] = jnp.zeros(s_ref.shape, F32)

    chunk = (nc - 1 - n) if rev else n
    rows = chunk * c + lax.broadcasted_iota(jnp.int32, (c, 1), 0)
    valid = rows >= N_PAD
    d = 1 if rev else 0
    nh = GDN_HEADS

    ab = ab_ref[...]
    g_all = -jnp.exp(prm_ref[0:1, :]) * _softplus(ab + prm_ref[1:2, :])
    g_all = jnp.where(valid, g_all, 0.0)
    beta_all = jnp.where(valid, _sigmoid(ab), 0.0)

    ri = lax.broadcasted_iota(jnp.int32, (c, c), 0)
    ci = lax.broadcasted_iota(jnp.int32, (c, c), 1)
    incl = (ri <= ci) if rev else (ri >= ci)
    strict = (ri < ci) if rev else (ri > ci)
    eye = (ri == ci).astype(F32)
    cum = _dot(incl.astype(F32), g_all, HIGHEST)
    cum_t = _dot_tn(cum, eye, HIGHEST)
    tot = cum[0:1, :] if rev else cum[c - 1:c, :]

    heads = range(nh)
    sls = [slice(h * GDN_DK, (h + 1) * GDN_DK) for h in heads]
    cols = [d * nh + h for h in heads]
    gcol = [cum[:, c_:c_ + 1] for c_ in cols]
    tot_h = [tot[:, c_:c_ + 1] for c_ in cols]
    bcol = [beta_all[:, 2 * nh + c_:2 * nh + c_ + 1] for c_ in cols]
    egc = [jnp.exp(g_) for g_ in gcol]
    qn, kn, kb, decay = [], [], [], []
    for h in heads:
        qh, kh = q_ref[:, sls[h]], k_ref[:, sls[h]]
        qn.append(qh * lax.rsqrt(jnp.sum(qh * qh, axis=-1, keepdims=True) + 1e-6) * (GDN_DK ** -0.5))
        kn.append(kh * lax.rsqrt(jnp.sum(kh * kh, axis=-1, keepdims=True) + 1e-6))
        kb.append(kn[h] * bcol[h])
        grow = cum_t[cols[h]:cols[h] + 1, :]
        decay.append(jnp.where(incl, jnp.exp(jnp.where(incl, gcol[h] - grow, 0.0)), 0.0))
    kq = [_bdot_nt(jnp.concatenate([kb[h], qn[h]], axis=0), kn[h]) for h in heads]
    lower = [jnp.where(strict, kq[h][:c] * decay[h], 0.0) for h in heads]
    qk = [kq[h][c:] * decay[h] for h in heads]
    x = [eye - lower[h] for h in heads]
    p = [_bdot(lower[h], lower[h]) for h in heads]
    n_sq = int(math.log2(c)) - 1
    for t in range(n_sq):
        x = [x[h] + _bdot(x[h], p[h]) for h in heads]
        if t + 1 < n_sq:
            p = [_bdot(p[h], p[h]) for h in heads]
    uw = [_bdot(x[h], jnp.concatenate([v_ref[:, sls[h]] * bcol[h], kb[h] * egc[h]], axis=1)) for h in heads]
    s_old = [s_ref[h] for h in heads]
    wq = [_bdot(jnp.concatenate([uw[h][:, GDN_DK:], qn[h] * egc[h]], axis=0), s_old[h]) for h in heads]
    v_new = [uw[h][:, :GDN_DK] - wq[h][:c] for h in heads]
    o = [wq[h][c:] + _bdot(qk[h], v_new[h]) for h in heads]
    kv = [_bdot_tn(kn[h] * jnp.exp(tot_h[h] - gcol[h]), v_new[h]) for h in heads]
    for h in heads:
        s_ref[h] = s_old[h] * jnp.exp(tot_h[h]) + kv[h]
        oh = o[h]
        if final:
            oh = oh + ob_ref[:, sls[h]]
            oh = oh * lax.rsqrt(jnp.mean(oh * oh, axis=-1, keepdims=True) + EPS) * nw_ref[...]
            oh = oh * _silu(z_ref[:, sls[h]])
        o_ref[:, sls[h]] = oh.astype(o_ref.dtype)


def _gdn_pass(qkv, ab, prm, *, rev, extra=None):
    bsz, l_len, _ = qkv.shape
    c = GDN_CHUNK
    nc = l_len // c
    wd = GDN_WIDTH
    ch = (lambda n: nc - 1 - n) if rev else (lambda n: n)
    in_specs = [pl.BlockSpec((None, c, wd), lambda b, n: (b, ch(n), 0)),
                pl.BlockSpec((None, c, wd), lambda b, n: (b, ch(n), 1)),
                pl.BlockSpec((None, c, wd), lambda b, n: (b, ch(n), 2)),
                pl.BlockSpec((None, c, LANE), lambda b, n: (b, ch(n), 0)),
                pl.BlockSpec((8, LANE), lambda b, n: (0, 0))]
    args = [qkv, qkv, qkv, ab, prm]
    final = extra is not None
    if final:
        o_other, proj, z_col0, norm_w = extra
        zoff = z_col0 // wd
        in_specs += [pl.BlockSpec((None, c, wd), lambda b, n: (b, ch(n), 0)),
                     pl.BlockSpec((None, c, wd), lambda b, n: (b, ch(n), zoff)),
                     pl.BlockSpec((1, GDN_DK), lambda b, n: (0, 0))]
        args += [o_other, proj, norm_w.reshape(1, GDN_DK)]
    return pl.pallas_call(
        functools.partial(_gdn_kernel, rev=rev, final=final, nc=nc),
        grid=(bsz, nc),
        in_specs=in_specs,
        out_specs=pl.BlockSpec((None, c, wd), lambda b, n: (b, ch(n), 0)),
        out_shape=jax.ShapeDtypeStruct((bsz, l_len, wd), BF16 if final else F32),
        scratch_shapes=[pltpu.VMEM((GDN_HEADS, GDN_DK, GDN_DK), F32)],
        compiler_params=_params("parallel", "arbitrary"),
        name="gdn_bwd" if rev else "gdn_fwd",
    )(*args)


def _diff_kernel(q_ref, k_ref, v_ref, lam_ref, nw_ref, o_ref, *, lam_init):
    l_len = k_ref.shape[0]
    key_ok = lax.broadcasted_iota(jnp.int32, (1, l_len), 1) >= N_PAD
    probs = []
    for m in range(2):
        sl = slice(m * DIFF_DQK, (m + 1) * DIFF_DQK)
        s = _dot_nt(q_ref[:, sl], k_ref[:, sl])
        s = jnp.where(key_ok, s, NEG)
        e = jnp.exp(s - jnp.max(s, axis=-1, keepdims=True))
        probs.append(e * (1.0 / jnp.sum(e, axis=-1, keepdims=True)))
    lv = lam_ref[...]
    lam = (jnp.exp(jnp.sum(lv[0:1] * lv[1:2], axis=-1, keepdims=True))
           - jnp.exp(jnp.sum(lv[2:3] * lv[3:4], axis=-1, keepdims=True)) + lam_init)
    wts = probs[0] - lam * probs[1]
    o = _dot(wts.astype(BF16), v_ref[...])
    o = o * lax.rsqrt(jnp.mean(o * o, axis=-1, keepdims=True) + 1e-5) * nw_ref[...]
    o_ref[...] = (o * (1.0 - lam_init)).astype(o_ref.dtype)


def _diff_attention(qkv, lam_vecs, norm_w, lam_init, *, tq):
    bsz, l_len, _ = qkv.shape
    wq = 2 * DIFF_DQK
    nhd = DIFF_HEADS
    return pl.pallas_call(
        functools.partial(_diff_kernel, lam_init=lam_init),
        grid=(bsz, nhd, l_len // tq),
        in_specs=[pl.BlockSpec((None, tq, wq), lambda b, h, i: (b, i, h)),
                  pl.BlockSpec((None, l_len, wq), lambda b, h, i: (b, 0, nhd + h)),
                  pl.BlockSpec((None, l_len, DIFF_DV), lambda b, h, i: (b, 0, 2 * nhd + h)),
                  pl.BlockSpec((4, DIFF_DQK), lambda b, h, i: (0, 0)),
                  pl.BlockSpec((1, DIFF_DV), lambda b, h, i: (0, 0))],
        out_specs=pl.BlockSpec((None, tq, DIFF_DV), lambda b, h, i: (b, i, h)),
        out_shape=jax.ShapeDtypeStruct((bsz, l_len, DIFF_WIDTH), BF16),
        compiler_params=_params("parallel", "parallel", "parallel"),
        name="diff_attention",
    )(qkv, qkv, qkv, lam_vecs, norm_w.reshape(1, DIFF_DV))


def _swa_kernel(sink_ref, q_ref, k_ref, v_ref, o_ref):
    l_len = k_ref.shape[0]
    blk = SWA_BLOCK
    band = 3 * blk
    kvh = pl.program_id(1)
    n = pl.program_id(2)
    start = pl.multiple_of(jnp.clip((n - 1) * blk, 0, l_len - band), blk)
    kb = k_ref[pl.ds(start, band), :]
    vb = v_ref[pl.ds(start, band), :]
    km = k_ref[0:LEAD, :]
    vm = v_ref[0:LEAD, :]
    q = jnp.concatenate([q_ref[:, g * HEAD_DIM:(g + 1) * HEAD_DIM] for g in range(SWA_GROUP)], axis=0)
    rows = SWA_GROUP * blk
    sb = _dot_nt(q, kb)
    sm = _dot_nt(q, km)
    qpos = n * blk + (lax.broadcasted_iota(jnp.int32, (rows, band), 0) & (blk - 1))
    kpos = start + lax.broadcasted_iota(jnp.int32, (rows, band), 1)
    band_ok = (jnp.abs(qpos - kpos) <= SWA_WINDOW) & (kpos >= LEAD)
    sb = jnp.where(band_ok, sb, NEG)
    sm = jnp.where(lax.broadcasted_iota(jnp.int32, (rows, LEAD), 1) >= N_PAD, sm, NEG)
    sk = jnp.concatenate([jnp.full((blk, 1), sink_ref[kvh * SWA_GROUP + g], F32) for g in range(SWA_GROUP)],
                         axis=0)
    mx = jnp.maximum(jnp.maximum(jnp.max(sb, axis=-1, keepdims=True), jnp.max(sm, axis=-1, keepdims=True)), sk)
    eb = jnp.exp(sb - mx)
    em = jnp.exp(sm - mx)
    den = jnp.sum(eb, axis=-1, keepdims=True) + jnp.sum(em, axis=-1, keepdims=True) + jnp.exp(sk - mx)
    o = (_dot(eb.astype(BF16), vb) + _dot(em.astype(BF16), vm)) * (1.0 / den)
    for g in range(SWA_GROUP):
        o_ref[:, g * HEAD_DIM:(g + 1) * HEAD_DIM] = o[g * blk:(g + 1) * blk].astype(o_ref.dtype)


def _window_attention(qkv, sink):
    bsz, l_len, _ = qkv.shape
    wq = SWA_GROUP * HEAD_DIM
    koff = SWA_WIDTH // HEAD_DIM
    voff = koff + SWA_KV_HEADS
    return pl.pallas_call(
        _swa_kernel,
        grid=(bsz, SWA_KV_HEADS, l_len // SWA_BLOCK),
        in_specs=[pl.BlockSpec(memory_space=pltpu.SMEM),
                  pl.BlockSpec((None, SWA_BLOCK, wq), lambda b, h, i: (b, i, h)),
                  pl.BlockSpec((None, l_len, HEAD_DIM), lambda b, h, i: (b, 0, koff + h)),
                  pl.BlockSpec((None, l_len, HEAD_DIM), lambda b, h, i: (b, 0, voff + h))],
        out_specs=pl.BlockSpec((None, SWA_BLOCK, wq), lambda b, h, i: (b, i, h)),
        out_shape=jax.ShapeDtypeStruct((bsz, l_len, SWA_WIDTH), BF16),
        compiler_params=_params("parallel", "parallel", "parallel"),
        name="window_attention",
    )(sink, qkv, qkv, qkv)


def _ssd_kernel(*refs, rev, final, nc):
    if final:
        x_ref, b_ref, c_ref, dt_ref, prm_ref, yb_ref, z0_ref, z1_ref, drow_ref, nw_ref, o_ref, s_ref = refs
    else:
        x_ref, b_ref, c_ref, dt_ref, prm_ref, o_ref, s_ref = refs
    q = SSD_CHUNK
    n = pl.program_id(1)

    @pl.when(n == 0)
    def _():
        s_ref[...] = jnp.zeros(s_ref.shape, F32)

    chunk = (nc - 1 - n) if rev else n
    rows = chunk * q + lax.broadcasted_iota(jnp.int32, (q, 1), 0)
    valid = rows >= N_PAD
    d = 1 if rev else 0
    p = SSD_HEADDIM
    gh = SSD_GROUP_HEADS
    gw = SSD_GROUP_WIDTH

    dt = jnp.where(valid, _softplus(dt_ref[...] + prm_ref[1:2, :]), 0.0)
    da = dt * (-jnp.exp(prm_ref[0:1, :]))
    ri = lax.broadcasted_iota(jnp.int32, (q, q), 0)
    ci = lax.broadcasted_iota(jnp.int32, (q, q), 1)
    incl = (ri <= ci) if rev else (ri >= ci)
    eye = (ri == ci).astype(F32)
    cum = _dot(incl.astype(F32), da, HIGHEST)
    cum_t = _dot_tn(cum, eye, HIGHEST)
    tot = cum[0:1, :] if rev else cum[q - 1:q, :]
    ecum = jnp.exp(cum)
    dstate = jnp.exp(tot - cum)
    etot = jnp.exp(tot)

    ys = []
    for gi in range(SSD_GROUPS):
        bg = b_ref[:, gi * SSD_STATE:(gi + 1) * SSD_STATE]
        cg = c_ref[:, gi * SSD_STATE:(gi + 1) * SSD_STATE]
        cb = _dot_nt(cg, bg)
        sg = s_ref[gi * gw:(gi + 1) * gw, :]
        y_off = _dot_nt(cg, sg)
        xs, decs = [], []
        for hh in range(gh):
            h = gi * gh + hh
            col = d * SSD_HEADS + h
            cumc = cum[:, col:col + 1]
            cumr = cum_t[col:col + 1, :]
            seg = jnp.where(incl, jnp.exp(jnp.where(incl, cumc - cumr, 0.0)), 0.0)
            xdt = x_ref[:, h * p:(h + 1) * p] * dt[:, col:col + 1]
            y_h = _dot(cb * seg, xdt) + y_off[:, hh * p:(hh + 1) * p] * ecum[:, col:col + 1]
            ys.append(y_h)
            xs.append(xdt * dstate[:, col:col + 1])
            decs.append(jnp.broadcast_to(etot[:, col:col + 1], (p, 1)))
        st = _dot_tn(jnp.concatenate(xs, axis=1), bg)
        s_ref[gi * gw:(gi + 1) * gw, :] = sg * jnp.concatenate(decs, axis=0) + st
    y = jnp.concatenate(ys, axis=1)
    if final:
        y = y + yb_ref[...] + x_ref[...] * drow_ref[...]
        y = y * _silu(jnp.concatenate([z0_ref[...], z1_ref[...]], axis=1))
        for gi in range(SSD_GROUPS):
            sl = slice(gi * gw, (gi + 1) * gw)
            yg = y[:, sl]
            yg = yg * lax.rsqrt(jnp.mean(yg * yg, axis=-1, keepdims=True) + EPS) * nw_ref[:, sl]
            o_ref[:, sl] = yg.astype(o_ref.dtype)
    else:
        o_ref[...] = y


def _ssd_pass(xbc, dt_raw, prm, *, rev, extra=None):
    bsz, l_len, _ = xbc.shape
    q = SSD_CHUNK
    nc = l_len // q
    wd = SSD_WIDTH
    gs = SSD_GROUPS * SSD_STATE
    ch = (lambda n: nc - 1 - n) if rev else (lambda n: n)
    in_specs = [pl.BlockSpec((None, q, wd), lambda b, n: (b, ch(n), 0)),
                pl.BlockSpec((None, q, gs), lambda b, n: (b, ch(n), wd // gs)),
                pl.BlockSpec((None, q, gs), lambda b, n: (b, ch(n), wd // gs + 1)),
                pl.BlockSpec((None, q, LANE), lambda b, n: (b, ch(n), 0)),
                pl.BlockSpec((8, LANE), lambda b, n: (0, 0))]
    args = [xbc, xbc, xbc, dt_raw, prm]
    final = extra is not None
    if final:
        y_other, proj, z_col0, d_row, norm_w = extra
        zw = wd // 2
        zoff = z_col0 // zw
        in_specs += [pl.BlockSpec((None, q, wd), lambda b, n: (b, ch(n), 0)),
                     pl.BlockSpec((None, q, zw), lambda b, n: (b, ch(n), zoff)),
                     pl.BlockSpec((None, q, zw), lambda b, n: (b, ch(n), zoff + 1)),
                     pl.BlockSpec((1, wd), lambda b, n: (0, 0)),
                     pl.BlockSpec((1, wd), lambda b, n: (0, 0))]
        args += [y_other, proj, proj, d_row, norm_w.reshape(1, wd)]
    return pl.pallas_call(
        functools.partial(_ssd_kernel, rev=rev, final=final, nc=nc),
        grid=(bsz, nc),
        in_specs=in_specs,
        out_specs=pl.BlockSpec((None, q, wd), lambda b, n: (b, ch(n), 0)),
        out_shape=jax.ShapeDtypeStruct((bsz, l_len, wd), BF16 if final else F32),
        scratch_shapes=[pltpu.VMEM((wd, SSD_STATE), F32)],
        compiler_params=_params("parallel", "arbitrary"),
        name="ssd_bwd" if rev else "ssd_fwd",
    )(*args)


def _lane_params(*rows):
    out = jnp.zeros((8, LANE), F32)
    for i, r in enumerate(rows):
        r = r.astype(F32).reshape(-1)
        out = out.at[i, :r.shape[0]].set(r)
    return out


def _pad_cols(w, n):
    return jnp.pad(w, ((0, 0), (0, n - w.shape[1])))


def _diff_lambda_init(layer):
    return 0.8 - 0.6 * math.exp(-0.3 * layer)


def _tiles(l_pad):
    tm = _tile(l_pad, 1056, 16)
    return tm


def _even_mixer(h, hn, w_in, conv_w, a_log, dt_bias, gdn_norm_w, lam_vecs, diff_norm_w, w_out, lam_init, tables):
    bsz, l_pad, d = h.shape
    tm = _tiles(l_pad)
    qkv_w = 3 * GDN_WIDTH
    main_w = qkv_w + GDN_WIDTH
    ab_w = 4 * GDN_HEADS
    w_tail = w_in[:, main_w + ab_w:].astype(BF16)
    w_ab = _pad_cols(w_in[:, main_w:main_w + ab_w], LANE).astype(BF16)
    proj_a = _matmul([(hn, w_in, 0, 0)], n=main_w, out_dtype=F32, tm=tm, tn=_tile(main_w, 512),
                     name="even_in_proj_a")
    proj_b = _matmul([(hn, w_tail, 0, 0)], out_dtype=F32, tm=tm, tn=_tile(w_tail.shape[1], 1024),
                     name="even_in_proj_b")
    ab = _matmul([(hn, w_ab, 0, 0)], out_dtype=F32, tm=tm, tn=LANE, name="even_ab_proj")

    qkv_a = _conv_silu(proj_a, 0, qkv_w, conv_w, None)
    prm = _lane_params(a_log, dt_bias)
    o_bwd = _gdn_pass(qkv_a, ab, prm, rev=True)
    o_a = _gdn_pass(qkv_a, ab, prm, rev=False, extra=(o_bwd, proj_a, qkv_w, gdn_norm_w))

    qk_w = 2 * DIFF_HEADS * DIFF_DQK
    cols = ((0, qk_w), (qk_w, qk_w), (2 * qk_w, DIFF_WIDTH))
    qkv_b = _rope_cast(proj_b, cols, tables, tr=_tile(l_pad, 384, 16))
    o_b = _diff_attention(qkv_b, lam_vecs, diff_norm_w, lam_init, tq=_tile(l_pad, 384, 16))

    return _matmul([(o_a, w_out, 0, 0), (o_b, w_out, GDN_WIDTH, 0)], out_dtype=F32, tm=tm,
                   tn=_tile(d, 512), res=h, zero_pad=True, name="even_out_proj")


def _odd_mixer(h, hn, w_in, sink, conv_w, conv_b, a_log, dt_bias, d_skip, ssd_norm_w, w_out, tables):
    bsz, l_pad, d = h.shape
    tm = _tiles(l_pad)
    kv_w = SWA_KV_HEADS * HEAD_DIM
    z_col = SWA_WIDTH + 2 * kv_w
    xbc_col = z_col + SSD_WIDTH
    main_w = xbc_col + SSD_XBC
    w_dt = _pad_cols(w_in[:, main_w:], LANE).astype(BF16)
    proj = _matmul([(hn, w_in, 0, 0)], n=main_w, out_dtype=F32, tm=tm, tn=_tile(main_w, 512), name="odd_in_proj")
    dt_raw = _matmul([(hn, w_dt, 0, 0)], out_dtype=F32, tm=tm, tn=LANE, name="odd_dt_proj")

    cols = ((0, SWA_WIDTH), (SWA_WIDTH, kv_w), (SWA_WIDTH + kv_w, kv_w))
    qkv_c = _rope_cast(proj, cols, tables, tr=_tile(l_pad, 384, 16))
    o_c = _window_attention(qkv_c, sink.astype(F32))

    xbc = _conv_silu(proj, xbc_col, SSD_XBC, conv_w, conv_b)
    prm = _lane_params(a_log, dt_bias)
    d_row = jnp.repeat(d_skip.astype(F32), SSD_HEADDIM).reshape(1, SSD_WIDTH)
    y_bwd = _ssd_pass(xbc, dt_raw, prm, rev=True)
    y = _ssd_pass(xbc, dt_raw, prm, rev=False, extra=(y_bwd, proj, z_col, d_row, ssd_norm_w))

    return _matmul([(o_c, w_out, 0, 0), (y, w_out, SWA_WIDTH, 0)], out_dtype=F32, tm=tm,
                   tn=_tile(d, 512), res=h, zero_pad=True, name="odd_out_proj")


def _ffn(h, norm_w, w_gate, w_up, w_down):
    bsz, l_pad, d = h.shape
    tm = _tiles(l_pad)
    f = w_gate.shape[1]
    hn = _rmsnorm(h, norm_w, out_dtype=BF16, tr=_tile(l_pad, 352, 16))
    act = _swiglu_up(hn, w_gate, w_up, tm=tm, tn=_tile(f, 256))
    nk = 2 if f > 4096 else 1
    return _matmul([(act, w_down.astype(BF16), 0, 0)], out_dtype=F32, tm=tm, tn=_tile(d, 512), nk=nk, res=h,
                   name="ffn_down")


def kernel(x, meta_tokens, norm_mix, norm_ffn, norm_final, even_w_in, even_conv, gdn_a_log, gdn_dt_bias, gdn_norm, diff_lam_q1, diff_lam_k1, diff_lam_q2, diff_lam_k2, diff_norm, even_w_out, odd_w_in, swa_sink, ssd_conv_w, ssd_conv_b, ssd_a_log, ssd_dt_bias, ssd_d, ssd_norm, odd_w_out, ffn_w_gate, ffn_w_up, ffn_w_down):
    bsz, seq, d = x.shape
    depth = norm_mix.shape[0]
    l_pad = LEAD + seq
    meta = jnp.broadcast_to(meta_tokens[None].astype(x.dtype), (bsz, N_META, d))
    h = jnp.concatenate([jnp.zeros((bsz, N_PAD, d), x.dtype), meta, x], axis=1)
    tables = _rope_tables(l_pad)
    tr = _tile(l_pad, 352, 16)
    for i in range(depth):
        j = i // 2
        hn = _rmsnorm(h, norm_mix[i], out_dtype=BF16, tr=tr)
        if i % 2 == 0:
            lam_vecs = jnp.stack([diff_lam_q1[j], diff_lam_k1[j], diff_lam_q2[j], diff_lam_k2[j]]).astype(F32)
            h = _even_mixer(h, hn, even_w_in[j], even_conv[j], gdn_a_log[j], gdn_dt_bias[j], gdn_norm[j],
                            lam_vecs, diff_norm[j], even_w_out[j], _diff_lambda_init(i), tables)
        else:
            h = _odd_mixer(h, hn, odd_w_in[j], swa_sink[j], ssd_conv_w[j], ssd_conv_b[j], ssd_a_log[j],
                           ssd_dt_bias[j], ssd_d[j], ssd_norm[j], odd_w_out[j], tables)
        h = _ffn(h, norm_ffn[i], ffn_w_gate[i], ffn_w_up[i], ffn_w_down[i])
    return _rmsnorm(h, norm_final, out_dtype=x.dtype, row0=LEAD, rows=seq, tr=LANE)
```

```python
import functools
import math

import jax
import jax.numpy as jnp
from jax import lax
from jax.experimental import pallas as pl
from jax.experimental.pallas import tpu as pltpu

F32 = jnp.float32
BF16 = jnp.bfloat16
HIGHEST = lax.Precision.HIGHEST

N_META = 16
LEAD = 128
N_PAD = LEAD - N_META
HEAD_DIM = 128
ROT_DIM = HEAD_DIM // 4
ROPE_THETA = 500000.0
CONV_W = 7
EPS = 1e-6

GDN_HEADS = 16
GDN_DK = 128
GDN_WIDTH = 2048
GDN_CHUNK = 64
DIFF_HEADS = 8
DIFF_DQK = 128
DIFF_DV = 256
DIFF_WIDTH = 2048
SWA_HEADS = 16
SWA_KV_HEADS = 4
SWA_GROUP = 4
SWA_WIDTH = 2048
SWA_WINDOW = 128
SWA_BLOCK = 128
SSD_WIDTH = 2048
SSD_HEADDIM = 64
SSD_HEADS = 32
SSD_STATE = 128
SSD_GROUPS = 4
SSD_GROUP_HEADS = SSD_HEADS // SSD_GROUPS
SSD_GROUP_WIDTH = SSD_WIDTH // SSD_GROUPS
SSD_XBC = 3072
SSD_CHUNK = 128

LANE = 128
NEG = -1e30
LOG2_E = math.log2(math.e)
VMEM_LIMIT = 56 * 1024 * 1024


def _params(*sem):
    return pltpu.CompilerParams(dimension_semantics=sem, vmem_limit_bytes=VMEM_LIMIT)


def _tile(n, pref, align=LANE):
    if n <= pref:
        return n
    t = (pref // align) * align
    while t >= align:
        if n % t == 0:
            return t
        t -= align
    return n


def _dot(a, b, precision=None):
    return jnp.dot(a, b, preferred_element_type=F32, precision=precision)


def _dot_nt(a, b, precision=None):
    return lax.dot_general(a, b, (((1,), (1,)), ((), ())), preferred_element_type=F32, precision=precision)


def _dot_tn(a, b, precision=None):
    return lax.dot_general(a, b, (((0,), (0,)), ((), ())), preferred_element_type=F32, precision=precision)


def _sigmoid(x):
    return 1.0 / (1.0 + jnp.exp(-x))


def _silu(x):
    return x * _sigmoid(x)


def _softplus(x):
    return jnp.maximum(x, 0.0) + jnp.log1p(jnp.exp(-jnp.abs(x)))


def _rmsnorm_kernel(x_ref, w_ref, o_ref, *, eps):
    x = x_ref[...]
    ms = jnp.mean(x * x, axis=-1, keepdims=True)
    o_ref[...] = (x * lax.rsqrt(ms + eps) * w_ref[...]).astype(o_ref.dtype)


def _rmsnorm(h, w, *, out_dtype, row0=0, rows=None, tr):
    bsz, l_len, d = h.shape
    rows = l_len - row0 if rows is None else rows
    off = row0 // tr
    return pl.pallas_call(
        functools.partial(_rmsnorm_kernel, eps=EPS),
        grid=(bsz, rows // tr),
        in_specs=[pl.BlockSpec((None, tr, d), lambda b, i: (b, i + off, 0)),
                  pl.BlockSpec((1, d), lambda b, i: (0, 0))],
        out_specs=pl.BlockSpec((None, tr, d), lambda b, i: (b, i, 0)),
        out_shape=jax.ShapeDtypeStruct((bsz, rows, d), out_dtype),
        compiler_params=_params("parallel", "parallel"),
        name="rmsnorm",
    )(h, w.reshape(1, d))


def _embed_kernel(x_ref, meta_ref, w_ref, h_ref, hn_ref, *, eps):
    def emit(blk):
        h_ref[...] = blk
        ms = jnp.mean(blk * blk, axis=-1, keepdims=True)
        hn_ref[...] = (blk * lax.rsqrt(ms + eps) * w_ref[...]).astype(hn_ref.dtype)

    i = pl.program_id(1)

    @pl.when(i == 0)
    def _():
        emit(jnp.concatenate([jnp.zeros((N_PAD, meta_ref.shape[-1]), F32), meta_ref[...]], axis=0))

    @pl.when(i > 0)
    def _():
        emit(x_ref[...])


def _embed_norm(x, meta, w):
    bsz, seq, d = x.shape
    l_pad = LEAD + seq
    blk = pl.BlockSpec((None, LEAD, d), lambda b, i: (b, i, 0))
    return pl.pallas_call(
        functools.partial(_embed_kernel, eps=EPS),
        grid=(bsz, l_pad // LEAD),
        in_specs=[pl.BlockSpec((None, LEAD, d), lambda b, i: (b, jnp.maximum(i - 1, 0), 0)),
                  pl.BlockSpec((N_META, d), lambda b, i: (0, 0)),
                  pl.BlockSpec((1, d), lambda b, i: (0, 0))],
        out_specs=[blk, blk],
        out_shape=[jax.ShapeDtypeStruct((bsz, l_pad, d), F32), jax.ShapeDtypeStruct((bsz, l_pad, d), BF16)],
        compiler_params=_params("parallel", "parallel"),
        name="embed_norm",
    )(x, meta.astype(F32), w.reshape(1, d))


def _tail_cast_kernel(x_ref, tail_ref, ab_ref, *, ab_w):
    tk, tail_w = tail_ref.shape
    tail_ref[...] = x_ref[:, ab_w:ab_w + tail_w].astype(tail_ref.dtype)
    ab = jnp.concatenate([x_ref[:, :ab_w], jnp.zeros((tk, LANE - ab_w), F32)], axis=1)
    ab_ref[...] = ab.astype(ab_ref.dtype)


def _tail_cast(w_in, layer, col0, ab_w, tail_w):
    _, kdim, n_total = w_in.shape
    win = col0
    assert col0 + ab_w + tail_w == n_total and ab_w + tail_w <= win
    tk = _tile(kdim, 256, 16)
    return pl.pallas_call(
        functools.partial(_tail_cast_kernel, ab_w=ab_w),
        grid=(kdim // tk,),
        in_specs=[pl.BlockSpec((None, tk, win), lambda k: (layer, k, 1))],
        out_specs=[pl.BlockSpec((None, tk, tail_w), lambda k: (0, k, 0)),
                   pl.BlockSpec((None, tk, LANE), lambda k: (0, k, 0))],
        out_shape=[jax.ShapeDtypeStruct((1, kdim, tail_w), BF16), jax.ShapeDtypeStruct((1, kdim, LANE), BF16)],
        compiler_params=_params("parallel"),
        name="tail_cast",
    )(w_in)


def _mm_kernel(*refs, n_pairs, nk, has_res, zero_pad, tm):
    ins = refs[:2 * n_pairs]
    pos = 2 * n_pairs
    res_ref = refs[pos] if has_res else None
    pos += int(has_res)
    o_ref = refs[pos]
    acc_ref = refs[pos + 1] if nk > 1 else None

    part = None
    for p in range(n_pairs):
        d = _dot(ins[2 * p][...], ins[2 * p + 1][...].astype(BF16))
        part = d if part is None else part + d

    def finish(acc):
        if has_res:
            acc = acc + res_ref[...]
        if zero_pad:
            row = pl.program_id(1) * tm + lax.broadcasted_iota(jnp.int32, acc.shape, 0)
            acc = jnp.where(row < N_PAD, 0.0, acc)
        o_ref[...] = acc.astype(o_ref.dtype)

    if nk == 1:
        finish(part)
    else:
        k = pl.program_id(3)

        @pl.when(k == 0)
        def _():
            acc_ref[...] = part

        @pl.when(k > 0)
        def _():
            acc_ref[...] += part

        @pl.when(k == nk - 1)
        def _():
            finish(acc_ref[...])


def _row_spec(block, index_map, single_buffer):
    if single_buffer:
        return pl.BlockSpec(block, index_map, pipeline_mode=pl.Buffered(1))
    return pl.BlockSpec(block, index_map)


def _matmul(pairs, *, out_dtype, tm, tn, n=None, nk=1, res=None, zero_pad=False, a_single=False, name):
    bsz, l_len, kdim = pairs[0][0].shape
    n = pairs[0][1].shape[2] if n is None else n
    tk = kdim // nk
    in_specs, args = [], []
    for a, b, layer, row0, col0 in pairs:
        assert row0 % tk == 0 and col0 % tn == 0
        in_specs += [_row_spec((None, tm, tk), lambda b_, i, j, k: (b_, i, k), a_single and nk == 1),
                     pl.BlockSpec((None, tk, tn),
                                  lambda b_, i, j, k, ly=layer, r=row0 // tk, c=col0 // tn: (ly, k + r, j + c))]
        args += [a, b]
    if res is not None:
        in_specs.append(pl.BlockSpec((None, tm, tn), lambda b_, i, j, k: (b_, i, j)))
        args.append(res)
    return pl.pallas_call(
        functools.partial(_mm_kernel, n_pairs=len(pairs), nk=nk, has_res=res is not None,
                          zero_pad=zero_pad, tm=tm),
        grid=(bsz, l_len // tm, n // tn, nk),
        in_specs=in_specs,
        out_specs=pl.BlockSpec((None, tm, tn), lambda b_, i, j, k: (b_, i, j)),
        out_shape=jax.ShapeDtypeStruct((bsz, l_len, n), out_dtype),
        scratch_shapes=[pltpu.VMEM((tm, tn), F32)] if nk > 1 else [],
        compiler_params=_params("parallel", "parallel", "parallel", "arbitrary"),
        name=name,
    )(*args)


def _swiglu_kernel(a_ref, g_ref, u_ref, o_ref):
    a = a_ref[...]
    g = _dot(a, g_ref[...].astype(BF16))
    u = _dot(a, u_ref[...].astype(BF16))
    o_ref[...] = (_silu(g) * u).astype(o_ref.dtype)


def _swiglu_up(a, w_gate, w_up, layer, *, tm, tn):
    bsz, l_len, kdim = a.shape
    n = w_gate.shape[2]
    return pl.pallas_call(
        _swiglu_kernel,
        grid=(bsz, l_len // tm, n // tn),
        in_specs=[_row_spec((None, tm, kdim), lambda b, i, j: (b, i, 0), True),
                  pl.BlockSpec((None, kdim, tn), lambda b, i, j: (layer, 0, j)),
                  pl.BlockSpec((None, kdim, tn), lambda b, i, j: (layer, 0, j))],
        out_specs=pl.BlockSpec((None, tm, tn), lambda b, i, j: (b, i, j)),
        out_shape=jax.ShapeDtypeStruct((bsz, l_len, n), BF16),
        compiler_params=_params("parallel", "parallel", "parallel"),
        name="swiglu_up",
    )(a, w_gate, w_up)


def _conv_kernel(*refs, l_len, rc, has_bias):
    if has_bias:
        x_ref, w_ref, b_ref, o_ref, xp_ref = refs
    else:
        x_ref, w_ref, o_ref, xp_ref = refs
        b_ref = None
    tc = x_ref.shape[-1]
    halo = 8
    xp_ref[0:halo, :] = jnp.zeros((halo, tc), F32)
    xp_ref[l_len + halo:l_len + 2 * halo, :] = jnp.zeros((halo, tc), F32)
    xp_ref[halo:l_len + halo, :] = x_ref[...]
    w = w_ref[...]
    reach = CONV_W // 2
    for c in range(l_len // rc):
        r0 = c * rc
        acc = None
        for j in range(CONV_W):
            s = r0 + halo - reach + j
            t = xp_ref[s:s + rc, :] * w[j:j + 1, :]
            acc = t if acc is None else acc + t
        if has_bias:
            acc = acc + b_ref[...]
        y = _silu(acc)
        if r0 < N_PAD:
            row = r0 + lax.broadcasted_iota(jnp.int32, y.shape, 0)
            y = jnp.where(row < N_PAD, 0.0, y)
        o_ref[r0:r0 + rc, :] = y


def _conv_silu(proj, col0, width, w, bias, *, tc=256, rc=128):
    bsz, l_len, _ = proj.shape
    off = col0 // tc
    in_specs = [pl.BlockSpec((None, l_len, tc), lambda b, j: (b, 0, j + off)),
                pl.BlockSpec((CONV_W, tc), lambda b, j: (0, j))]
    args = [proj, w]
    if bias is not None:
        in_specs.append(pl.BlockSpec((1, tc), lambda b, j: (0, j)))
        args.append(bias.reshape(1, width))
    return pl.pallas_call(
        functools.partial(_conv_kernel, l_len=l_len, rc=rc, has_bias=bias is not None),
        grid=(bsz, width // tc),
        in_specs=in_specs,
        out_specs=pl.BlockSpec((None, l_len, tc), lambda b, j: (b, 0, j)),
        out_shape=jax.ShapeDtypeStruct((bsz, l_len, width), F32),
        scratch_shapes=[pltpu.VMEM((l_len + 16, tc), F32)],
        compiler_params=_params("parallel", "parallel"),
        name="conv_silu",
    )(*args)


def _rope(x, cf, s1, s2):
    half = ROT_DIM // 2
    return x * cf + pltpu.roll(x, LANE - half, 1) * s1 + pltpu.roll(x, half, 1) * s2


def _rope_cast_kernel(q_ref, k_ref, v_ref, cf_ref, s1_ref, s2_ref, o_ref, *, q_scale):
    cf, s1, s2 = cf_ref[...], s1_ref[...], s2_ref[...]
    wq, wk, wv = q_ref.shape[-1], k_ref.shape[-1], v_ref.shape[-1]
    for hd in range(wq // HEAD_DIM):
        sl = slice(hd * HEAD_DIM, (hd + 1) * HEAD_DIM)
        o_ref[:, sl] = (_rope(q_ref[:, sl], cf, s1, s2) * q_scale).astype(o_ref.dtype)
    for hd in range(wk // HEAD_DIM):
        sl = slice(hd * HEAD_DIM, (hd + 1) * HEAD_DIM)
        o_ref[:, wq + hd * HEAD_DIM:wq + (hd + 1) * HEAD_DIM] = _rope(k_ref[:, sl], cf, s1, s2).astype(o_ref.dtype)
    o_ref[:, wq + wk:wq + wk + wv] = v_ref[...].astype(o_ref.dtype)


def _rope_cast(proj, cols, tables, *, tr, q_scale):
    bsz, l_len, _ = proj.shape
    width = sum(w for _, w in cols)
    tab_spec = pl.BlockSpec((tr, LANE), lambda b, i: (i, 0))

    def col_spec(col0, w):
        return pl.BlockSpec((None, tr, w), lambda b, i: (b, i, col0 // w))

    return pl.pallas_call(
        functools.partial(_rope_cast_kernel, q_scale=q_scale),
        grid=(bsz, l_len // tr),
        in_specs=[col_spec(*c) for c in cols] + [tab_spec, tab_spec, tab_spec],
        out_specs=pl.BlockSpec((None, tr, width), lambda b, i: (b, i, 0)),
        out_shape=jax.ShapeDtypeStruct((bsz, l_len, width), BF16),
        compiler_params=_params("parallel", "parallel"),
        name="rope_cast",
    )(proj, proj, proj, *tables)


def _rope_tables(l_pad):
    half = ROT_DIM // 2
    inv = ROPE_THETA ** (-jnp.arange(0, ROT_DIM, 2, dtype=F32) / ROT_DIM)
    pos = (jnp.arange(l_pad) - N_PAD).astype(F32)
    ang = pos[:, None] * inv[None, :]
    cos, sin = jnp.cos(ang), jnp.sin(ang)
    ones = jnp.ones((l_pad, LANE - ROT_DIM), F32)
    zeros = jnp.zeros((l_pad, LANE - ROT_DIM), F32)
    zh = jnp.zeros((l_pad, half), F32)
    cf = jnp.concatenate([cos, cos, ones], axis=1)
    s1 = jnp.concatenate([-sin, zh, zeros], axis=1)
    s2 = jnp.concatenate([zh, sin, zeros], axis=1)
    return cf, s1, s2


def _bdot(a, b):
    return _dot(a.astype(BF16), b.astype(BF16))


def _bdot_nt(a, b):
    return _dot_nt(a.astype(BF16), b.astype(BF16))


def _bdot_tn(a, b):
    return _dot_tn(a.astype(BF16), b.astype(BF16))


def _gdn_kernel(*refs, rev, final, nc):
    if final:
        q_ref, k_ref, v_ref, ab_ref, prm_ref, ob_ref, z_ref, nw_ref, o_ref, s_ref = refs
    else:
        q_ref, k_ref, v_ref, ab_ref, prm_ref, o_ref, s_ref = refs
    c = GDN_CHUNK
    n = pl.program_id(1)

    @pl.when(n == 0)
    def _():
        s_ref[...] = jnp.zeros(s_ref.shape, F32)

    chunk = (nc - 1 - n) if rev else n
    rows = chunk * c + lax.broadcasted_iota(jnp.int32, (c, 1), 0)
    valid = rows >= N_PAD
    d = 1 if rev else 0
    nh = GDN_HEADS

    ab = ab_ref[...]
    g_all = -jnp.exp(prm_ref[0:1, :]) * _softplus(ab + prm_ref[1:2, :])
    g_all = jnp.where(valid, g_all, 0.0)
    beta_all = jnp.where(valid, _sigmoid(ab), 0.0)

    ri = lax.broadcasted_iota(jnp.int32, (c, c), 0)
    ci = lax.broadcasted_iota(jnp.int32, (c, c), 1)
    incl = (ri <= ci) if rev else (ri >= ci)
    strict = (ri < ci) if rev else (ri > ci)
    eye = (ri == ci).astype(F32)
    cum = _dot(incl.astype(F32), g_all, HIGHEST)
    cum_t = _dot_tn(cum, eye, HIGHEST)
    tot = cum[0:1, :] if rev else cum[c - 1:c, :]

    heads = range(nh)
    sls = [slice(h * GDN_DK, (h + 1) * GDN_DK) for h in heads]
    cols = [d * nh + h for h in heads]
    gcol = [cum[:, c_:c_ + 1] for c_ in cols]
    tot_h = [tot[:, c_:c_ + 1] for c_ in cols]
    bcol = [beta_all[:, 2 * nh + c_:2 * nh + c_ + 1] for c_ in cols]
    egc = [jnp.exp(g_) for g_ in gcol]
    qn, kn, kb, decay = [], [], [], []
    for h in heads:
        qh, kh = q_ref[:, sls[h]], k_ref[:, sls[h]]
        qn.append(qh * lax.rsqrt(jnp.sum(qh * qh, axis=-1, keepdims=True) + 1e-6) * (GDN_DK ** -0.5))
        kn.append(kh * lax.rsqrt(jnp.sum(kh * kh, axis=-1, keepdims=True) + 1e-6))
        kb.append(kn[h] * bcol[h])
        grow = cum_t[cols[h]:cols[h] + 1, :]
        decay.append(jnp.where(incl, jnp.exp(jnp.where(incl, gcol[h] - grow, 0.0)), 0.0))
    kq = [_bdot_nt(jnp.concatenate([kb[h], qn[h]], axis=0), kn[h]) for h in heads]
    lower = [jnp.where(strict, kq[h][:c] * decay[h], 0.0) for h in heads]
    qk = [kq[h][c:] * decay[h] for h in heads]
    x = [eye - lower[h] for h in heads]
    p = [_bdot(lower[h], lower[h]) for h in heads]
    n_sq = int(math.log2(c)) - 1
    for t in range(n_sq):
        x = [x[h] + _bdot(x[h], p[h]) for h in heads]
        if t + 1 < n_sq:
            p = [_bdot(p[h], p[h]) for h in heads]
    uw = [_bdot(x[h], jnp.concatenate([v_ref[:, sls[h]] * bcol[h], kb[h] * egc[h]], axis=1)) for h in heads]
    s_old = [s_ref[h] for h in heads]
    wq = [_bdot(jnp.concatenate([uw[h][:, GDN_DK:], qn[h] * egc[h]], axis=0), s_old[h]) for h in heads]
    v_new = [uw[h][:, :GDN_DK] - wq[h][:c] for h in heads]
    o = [wq[h][c:] + _bdot(qk[h], v_new[h]) for h in heads]
    kv = [_bdot_tn(kn[h] * jnp.exp(tot_h[h] - gcol[h]), v_new[h]) for h in heads]
    for h in heads:
        s_ref[h] = s_old[h] * jnp.exp(tot_h[h]) + kv[h]
        oh = o[h]
        if final:
            oh = oh + ob_ref[:, sls[h]]
            oh = oh * lax.rsqrt(jnp.mean(oh * oh, axis=-1, keepdims=True) + EPS) * nw_ref[...]
            oh = oh * _silu(z_ref[:, sls[h]])
        o_ref[:, sls[h]] = oh.astype(o_ref.dtype)


def _gdn_pass(qkv, ab, prm, *, rev, extra=None):
    bsz, l_len, _ = qkv.shape
    c = GDN_CHUNK
    nc = l_len // c
    wd = GDN_WIDTH
    ch = (lambda n: nc - 1 - n) if rev else (lambda n: n)
    in_specs = [pl.BlockSpec((None, c, wd), lambda b, n: (b, ch(n), 0)),
                pl.BlockSpec((None, c, wd), lambda b, n: (b, ch(n), 1)),
                pl.BlockSpec((None, c, wd), lambda b, n: (b, ch(n), 2)),
                pl.BlockSpec((None, c, LANE), lambda b, n: (b, ch(n), 0)),
                pl.BlockSpec((8, LANE), lambda b, n: (0, 0))]
    args = [qkv, qkv, qkv, ab, prm]
    final = extra is not None
    if final:
        o_other, proj, z_col0, norm_w = extra
        zoff = z_col0 // wd
        in_specs += [pl.BlockSpec((None, c, wd), lambda b, n: (b, ch(n), 0)),
                     pl.BlockSpec((None, c, wd), lambda b, n: (b, ch(n), zoff)),
                     pl.BlockSpec((1, GDN_DK), lambda b, n: (0, 0))]
        args += [o_other, proj, norm_w.reshape(1, GDN_DK)]
    return pl.pallas_call(
        functools.partial(_gdn_kernel, rev=rev, final=final, nc=nc),
        grid=(bsz, nc),
        in_specs=in_specs,
        out_specs=pl.BlockSpec((None, c, wd), lambda b, n: (b, ch(n), 0)),
        out_shape=jax.ShapeDtypeStruct((bsz, l_len, wd), BF16 if final else F32),
        scratch_shapes=[pltpu.VMEM((GDN_HEADS, GDN_DK, GDN_DK), F32)],
        compiler_params=_params("parallel", "arbitrary"),
        name="gdn_bwd" if rev else "gdn_fwd",
    )(*args)


def _diff_kernel(q_ref, k_ref, v_ref, lam_ref, nw_ref, o_ref, *, lam_init):
    key_ok = lax.broadcasted_iota(jnp.int32, (1, LEAD), 1) >= N_PAD
    v = v_ref[...]
    outs = []
    for m in range(2):
        sl = slice(m * DIFF_DQK, (m + 1) * DIFF_DQK)
        s = _dot_nt(q_ref[:, sl], k_ref[:, sl])
        s = jnp.concatenate([jnp.where(key_ok, s[:, :LEAD], NEG), s[:, LEAD:]], axis=1)
        e = jnp.exp2(s - jnp.max(s, axis=-1, keepdims=True))
        den = jnp.sum(e, axis=-1, keepdims=True)
        outs.append(_dot(e.astype(BF16), v) * (1.0 / den))
    lv = lam_ref[...]
    lam = (jnp.exp(jnp.sum(lv[0:1] * lv[1:2], axis=-1, keepdims=True))
           - jnp.exp(jnp.sum(lv[2:3] * lv[3:4], axis=-1, keepdims=True)) + lam_init)
    o = outs[0] - lam * outs[1]
    o = o * lax.rsqrt(jnp.mean(o * o, axis=-1, keepdims=True) + 1e-5) * nw_ref[...]
    o_ref[...] = (o * (1.0 - lam_init)).astype(o_ref.dtype)


def _diff_attention(qkv, lam_vecs, norm_w, lam_init, *, tq):
    bsz, l_len, _ = qkv.shape
    wq = 2 * DIFF_DQK
    nhd = DIFF_HEADS
    return pl.pallas_call(
        functools.partial(_diff_kernel, lam_init=lam_init),
        grid=(bsz, nhd, l_len // tq),
        in_specs=[pl.BlockSpec((None, tq, wq), lambda b, h, i: (b, i, h)),
                  pl.BlockSpec((None, l_len, wq), lambda b, h, i: (b, 0, nhd + h)),
                  pl.BlockSpec((None, l_len, DIFF_DV), lambda b, h, i: (b, 0, 2 * nhd + h)),
                  pl.BlockSpec((4, DIFF_DQK), lambda b, h, i: (0, 0)),
                  pl.BlockSpec((1, DIFF_DV), lambda b, h, i: (0, 0))],
        out_specs=pl.BlockSpec((None, tq, DIFF_DV), lambda b, h, i: (b, i, h)),
        out_shape=jax.ShapeDtypeStruct((bsz, l_len, DIFF_WIDTH), BF16),
        compiler_params=_params("parallel", "parallel", "parallel"),
        name="diff_attention",
    )(qkv, qkv, qkv, lam_vecs, norm_w.reshape(1, DIFF_DV))


def _swa_kernel(sink_ref, q_ref, k_ref, v_ref, o_ref):
    l_len = k_ref.shape[0]
    blk = SWA_BLOCK
    band = 3 * blk
    kvh = pl.program_id(1)
    n = pl.program_id(2)
    start = pl.multiple_of(jnp.clip((n - 1) * blk, 0, l_len - band), blk)
    kb = k_ref[pl.ds(start, band), :]
    vb = v_ref[pl.ds(start, band), :]
    km = k_ref[0:LEAD, :]
    vm = v_ref[0:LEAD, :]
    q = jnp.concatenate([q_ref[:, g * HEAD_DIM:(g + 1) * HEAD_DIM] for g in range(SWA_GROUP)], axis=0)
    rows = SWA_GROUP * blk
    sb = _dot_nt(q, kb)
    sm = _dot_nt(q, km)
    qpos = n * blk + (lax.broadcasted_iota(jnp.int32, (rows, band), 0) & (blk - 1))
    kpos = start + lax.broadcasted_iota(jnp.int32, (rows, band), 1)
    band_ok = (jnp.abs(qpos - kpos) <= SWA_WINDOW) & (kpos >= LEAD)
    sb = jnp.where(band_ok, sb, NEG)
    sm = jnp.where(lax.broadcasted_iota(jnp.int32, (rows, LEAD), 1) >= N_PAD, sm, NEG)
    sk = jnp.concatenate([jnp.full((blk, 1), sink_ref[kvh * SWA_GROUP + g], F32) for g in range(SWA_GROUP)],
                         axis=0)
    mx = jnp.maximum(jnp.maximum(jnp.max(sb, axis=-1, keepdims=True), jnp.max(sm, axis=-1, keepdims=True)), sk)
    eb = jnp.exp(sb - mx)
    em = jnp.exp(sm - mx)
    den = jnp.sum(eb, axis=-1, keepdims=True) + jnp.sum(em, axis=-1, keepdims=True) + jnp.exp(sk - mx)
    o = (_dot(eb.astype(BF16), vb) + _dot(em.astype(BF16), vm)) * (1.0 / den)
    for g in range(SWA_GROUP):
        o_ref[:, g * HEAD_DIM:(g + 1) * HEAD_DIM] = o[g * blk:(g + 1) * blk].astype(o_ref.dtype)


def _window_attention(qkv, sink):
    bsz, l_len, _ = qkv.shape
    wq = SWA_GROUP * HEAD_DIM
    koff = SWA_WIDTH // HEAD_DIM
    voff = koff + SWA_KV_HEADS
    return pl.pallas_call(
        _swa_kernel,
        grid=(bsz, SWA_KV_HEADS, l_len // SWA_BLOCK),
        in_specs=[pl.BlockSpec(memory_space=pltpu.SMEM),
                  pl.BlockSpec((None, SWA_BLOCK, wq), lambda b, h, i: (b, i, h)),
                  pl.BlockSpec((None, l_len, HEAD_DIM), lambda b, h, i: (b, 0, koff + h)),
                  pl.BlockSpec((None, l_len, HEAD_DIM), lambda b, h, i: (b, 0, voff + h))],
        out_specs=pl.BlockSpec((None, SWA_BLOCK, wq), lambda b, h, i: (b, i, h)),
        out_shape=jax.ShapeDtypeStruct((bsz, l_len, SWA_WIDTH), BF16),
        compiler_params=_params("parallel", "parallel", "parallel"),
        name="window_attention",
    )(sink, qkv, qkv, qkv)


def _ssd_kernel(*refs, rev, final, nc):
    if final:
        x_ref, b_ref, c_ref, dt_ref, prm_ref, yb_ref, z0_ref, z1_ref, drow_ref, nw_ref, o_ref, s_ref = refs
    else:
        x_ref, b_ref, c_ref, dt_ref, prm_ref, o_ref, s_ref = refs
    q = SSD_CHUNK
    n = pl.program_id(1)

    @pl.when(n == 0)
    def _():
        s_ref[...] = jnp.zeros(s_ref.shape, F32)

    chunk = (nc - 1 - n) if rev else n
    rows = chunk * q + lax.broadcasted_iota(jnp.int32, (q, 1), 0)
    valid = rows >= N_PAD
    d = 1 if rev else 0
    p = SSD_HEADDIM
    gh = SSD_GROUP_HEADS
    gw = SSD_GROUP_WIDTH

    dt = jnp.where(valid, _softplus(dt_ref[...] + prm_ref[1:2, :]), 0.0)
    da = dt * (-jnp.exp(prm_ref[0:1, :]))
    ri = lax.broadcasted_iota(jnp.int32, (q, q), 0)
    ci = lax.broadcasted_iota(jnp.int32, (q, q), 1)
    incl = (ri <= ci) if rev else (ri >= ci)
    eye = (ri == ci).astype(F32)
    cum = _dot(incl.astype(F32), da, HIGHEST)
    cum_t = _dot_tn(cum, eye, HIGHEST)
    tot = cum[0:1, :] if rev else cum[q - 1:q, :]
    ecum = jnp.exp(cum)
    dstate = jnp.exp(tot - cum)
    etot = jnp.exp(tot)

    lane_lo = lax.broadcasted_iota(jnp.int32, (1, LANE), 1) < p
    groups = range(SSD_GROUPS)
    pairs = range(SSD_HEADS // 2)
    ppg = gh // 2
    col0 = [d * SSD_HEADS + 2 * j for j in pairs]

    def pair_cols(arr, j):
        return jnp.where(lane_lo, arr[:, col0[j]:col0[j] + 1], arr[:, col0[j] + 1:col0[j] + 2])

    def seg(col):
        return jnp.where(incl, jnp.exp(jnp.where(incl, cum[:, col:col + 1] - cum_t[col:col + 1, :], 0.0)), 0.0)

    bg = [b_ref[:, g * SSD_STATE:(g + 1) * SSD_STATE].astype(BF16) for g in groups]
    cg = [c_ref[:, g * SSD_STATE:(g + 1) * SSD_STATE].astype(BF16) for g in groups]
    sg = [s_ref[g * gw:(g + 1) * gw, :] for g in groups]
    cb = [_dot_nt(cg[g], bg[g]) for g in groups]
    y_off = [_dot_nt(cg[g], sg[g].astype(BF16)) for g in groups]
    xdt = [x_ref[:, j * LANE:(j + 1) * LANE] * pair_cols(dt, j) for j in pairs]
    lhs = [jnp.concatenate([cb[j // ppg] * seg(col0[j]), cb[j // ppg] * seg(col0[j] + 1)], axis=1) for j in pairs]
    rhs = [jnp.concatenate([jnp.where(lane_lo, xdt[j], 0.0), jnp.where(lane_lo, 0.0, xdt[j])], axis=0)
           for j in pairs]
    y_diag = [_bdot(lhs[j], rhs[j]) for j in pairs]
    ys = [y_diag[j] + y_off[j // ppg][:, (j % ppg) * LANE:(j % ppg + 1) * LANE] * pair_cols(ecum, j) for j in pairs]
    xs = [xdt[j] * pair_cols(dstate, j) for j in pairs]
    st = [_bdot_tn(jnp.concatenate(xs[g * ppg:(g + 1) * ppg], axis=1), bg[g]) for g in groups]
    for g in groups:
        decs = [jnp.broadcast_to(etot[:, c_:c_ + 1], (p, 1))
                for c_ in range(d * SSD_HEADS + g * gh, d * SSD_HEADS + (g + 1) * gh)]
        s_ref[g * gw:(g + 1) * gw, :] = sg[g] * jnp.concatenate(decs, axis=0) + st[g]
    y = jnp.concatenate(ys, axis=1)
    if final:
        y = y + yb_ref[...] + x_ref[...] * drow_ref[...]
        y = y * _silu(jnp.concatenate([z0_ref[...], z1_ref[...]], axis=1))
        for gi in range(SSD_GROUPS):
            sl = slice(gi * gw, (gi + 1) * gw)
            yg = y[:, sl]
            yg = yg * lax.rsqrt(jnp.mean(yg * yg, axis=-1, keepdims=True) + EPS) * nw_ref[:, sl]
            o_ref[:, sl] = yg.astype(o_ref.dtype)
    else:
        o_ref[...] = y


def _ssd_pass(xbc, dt_raw, prm, *, rev, extra=None):
    bsz, l_len, _ = xbc.shape
    q = SSD_CHUNK
    nc = l_len // q
    wd = SSD_WIDTH
    gs = SSD_GROUPS * SSD_STATE
    ch = (lambda n: nc - 1 - n) if rev else (lambda n: n)
    in_specs = [pl.BlockSpec((None, q, wd), lambda b, n: (b, ch(n), 0)),
                pl.BlockSpec((None, q, gs), lambda b, n: (b, ch(n), wd // gs)),
                pl.BlockSpec((None, q, gs), lambda b, n: (b, ch(n), wd // gs + 1)),
                pl.BlockSpec((None, q, LANE), lambda b, n: (b, ch(n), 0)),
                pl.BlockSpec((8, LANE), lambda b, n: (0, 0))]
    args = [xbc, xbc, xbc, dt_raw, prm]
    final = extra is not None
    if final:
        y_other, proj, z_col0, d_row, norm_w = extra
        zw = wd // 2
        zoff = z_col0 // zw
        in_specs += [pl.BlockSpec((None, q, wd), lambda b, n: (b, ch(n), 0)),
                     pl.BlockSpec((None, q, zw), lambda b, n: (b, ch(n), zoff)),
                     pl.BlockSpec((None, q, zw), lambda b, n: (b, ch(n), zoff + 1)),
                     pl.BlockSpec((1, wd), lambda b, n: (0, 0)),
                     pl.BlockSpec((1, wd), lambda b, n: (0, 0))]
        args += [y_other, proj, proj, d_row, norm_w.reshape(1, wd)]
    return pl.pallas_call(
        functools.partial(_ssd_kernel, rev=rev, final=final, nc=nc),
        grid=(bsz, nc),
        in_specs=in_specs,
        out_specs=pl.BlockSpec((None, q, wd), lambda b, n: (b, ch(n), 0)),
        out_shape=jax.ShapeDtypeStruct((bsz, l_len, wd), BF16 if final else F32),
        scratch_shapes=[pltpu.VMEM((wd, SSD_STATE), F32)],
        compiler_params=_params("parallel", "arbitrary"),
        name="ssd_bwd" if rev else "ssd_fwd",
    )(*args)


def _lane_params(*rows):
    out = jnp.zeros((8, LANE), F32)
    for i, r in enumerate(rows):
        r = r.astype(F32).reshape(-1)
        out = out.at[i, :r.shape[0]].set(r)
    return out


def _pad_cols(w, n):
    return jnp.pad(w, ((0, 0), (0, n - w.shape[1])))


def _diff_lambda_init(layer):
    return 0.8 - 0.6 * math.exp(-0.3 * layer)


def _tiles(l_pad):
    return _tile(l_pad, 1056, 16)


def _tall_tile(l_pad):
    return _tile(l_pad, 2112, 16)


def _even_mixer(h, hn, w_in, j, conv_w, a_log, dt_bias, gdn_norm_w, lam_vecs, diff_norm_w, w_out, lam_init, tables):
    bsz, l_pad, d = h.shape
    tm = _tiles(l_pad)
    qkv_w = 3 * GDN_WIDTH
    main_w = qkv_w + GDN_WIDTH
    ab_w = 4 * GDN_HEADS
    w_tail, w_ab = _tail_cast(w_in, j, main_w, ab_w, w_in.shape[2] - main_w - ab_w)
    tall = _tall_tile(l_pad)
    proj_a = _matmul([(hn, w_in, j, 0, 0)], n=main_w, out_dtype=F32, tm=tall, tn=_tile(main_w, 512),
                     a_single=True, name="even_in_proj_a")
    proj_b = _matmul([(hn, w_tail, 0, 0, 0)], out_dtype=F32, tm=tall, tn=_tile(w_tail.shape[2], 512),
                     a_single=True, name="even_in_proj_b")
    ab = _matmul([(hn, w_ab, 0, 0, 0)], out_dtype=F32, tm=tm, tn=LANE, name="even_ab_proj")

    qkv_a = _conv_silu(proj_a, 0, qkv_w, conv_w, None)
    prm = _lane_params(a_log, dt_bias)
    o_bwd = _gdn_pass(qkv_a, ab, prm, rev=True)
    o_a = _gdn_pass(qkv_a, ab, prm, rev=False, extra=(o_bwd, proj_a, qkv_w, gdn_norm_w))

    qk_w = 2 * DIFF_HEADS * DIFF_DQK
    cols = ((0, qk_w), (qk_w, qk_w), (2 * qk_w, DIFF_WIDTH))
    qkv_b = _rope_cast(proj_b, cols, tables, tr=_tile(l_pad, 384, 16), q_scale=DIFF_DQK ** -0.5 * LOG2_E)
    o_b = _diff_attention(qkv_b, lam_vecs, diff_norm_w, lam_init, tq=_tile(l_pad, 384, 16))

    return _matmul([(o_a, w_out, j, 0, 0), (o_b, w_out, j, GDN_WIDTH, 0)], out_dtype=F32, tm=tm,
                   tn=_tile(d, 512), res=h, zero_pad=True, name="even_out_proj")


def _odd_mixer(h, hn, w_in, j, sink, conv_w, conv_b, a_log, dt_bias, d_skip, ssd_norm_w, w_out, tables):
    bsz, l_pad, d = h.shape
    tm = _tiles(l_pad)
    kv_w = SWA_KV_HEADS * HEAD_DIM
    z_col = SWA_WIDTH + 2 * kv_w
    xbc_col = z_col + SSD_WIDTH
    main_w = xbc_col + SSD_XBC
    w_dt = _pad_cols(w_in[j, :, main_w:], LANE).astype(BF16)[None]
    proj = _matmul([(hn, w_in, j, 0, 0)], n=main_w, out_dtype=F32, tm=_tall_tile(l_pad), tn=_tile(main_w, 512),
                   a_single=True, name="odd_in_proj")
    dt_raw = _matmul([(hn, w_dt, 0, 0, 0)], out_dtype=F32, tm=tm, tn=LANE, name="odd_dt_proj")

    cols = ((0, SWA_WIDTH), (SWA_WIDTH, kv_w), (SWA_WIDTH + kv_w, kv_w))
    qkv_c = _rope_cast(proj, cols, tables, tr=_tile(l_pad, 384, 16), q_scale=HEAD_DIM ** -0.5)
    o_c = _window_attention(qkv_c, sink.astype(F32))

    xbc = _conv_silu(proj, xbc_col, SSD_XBC, conv_w, conv_b)
    prm = _lane_params(a_log, dt_bias)
    d_row = jnp.repeat(d_skip.astype(F32), SSD_HEADDIM).reshape(1, SSD_WIDTH)
    y_bwd = _ssd_pass(xbc, dt_raw, prm, rev=True)
    y = _ssd_pass(xbc, dt_raw, prm, rev=False, extra=(y_bwd, proj, z_col, d_row, ssd_norm_w))

    return _matmul([(o_c, w_out, j, 0, 0), (y, w_out, j, SWA_WIDTH, 0)], out_dtype=F32, tm=tm,
                   tn=_tile(d, 512), res=h, zero_pad=True, name="odd_out_proj")


def _ffn(h, norm_w, w_gate, w_up, w_down_bf16, layer):
    bsz, l_pad, d = h.shape
    f = w_gate.shape[2]
    hn = _rmsnorm(h, norm_w, out_dtype=BF16, tr=_tile(l_pad, 352, 16))
    act = _swiglu_up(hn, w_gate, w_up, layer, tm=_tall_tile(l_pad), tn=_tile(f, 256))
    return _matmul([(act, w_down_bf16, layer, 0, 0)], out_dtype=F32, tm=_tile(l_pad, 704, 16), tn=_tile(d, 256),
                   res=h, name="ffn_down")


def kernel(x, meta_tokens, norm_mix, norm_ffn, norm_final, even_w_in, even_conv, gdn_a_log, gdn_dt_bias, gdn_norm, diff_lam_q1, diff_lam_k1, diff_lam_q2, diff_lam_k2, diff_norm, even_w_out, odd_w_in, swa_sink, ssd_conv_w, ssd_conv_b, ssd_a_log, ssd_dt_bias, ssd_d, ssd_norm, odd_w_out, ffn_w_gate, ffn_w_up, ffn_w_down):
    bsz, seq, d = x.shape
    depth = norm_mix.shape[0]
    l_pad = LEAD + seq
    tables = _rope_tables(l_pad)
    tr = _tile(l_pad, 352, 16)
    w_down_bf16 = ffn_w_down.astype(BF16)
    for i in range(depth):
        j = i // 2
        if i == 0:
            h, hn = _embed_norm(x, meta_tokens, norm_mix[0])
        else:
            hn = _rmsnorm(h, norm_mix[i], out_dtype=BF16, tr=tr)
        if i % 2 == 0:
            lam_vecs = jnp.stack([diff_lam_q1[j], diff_lam_k1[j], diff_lam_q2[j], diff_lam_k2[j]]).astype(F32)
            h = _even_mixer(h, hn, even_w_in, j, even_conv[j], gdn_a_log[j], gdn_dt_bias[j], gdn_norm[j],
                            lam_vecs, diff_norm[j], even_w_out, _diff_lambda_init(i), tables)
        else:
            h = _odd_mixer(h, hn, odd_w_in, j, swa_sink[j], ssd_conv_w[j], ssd_conv_b[j], ssd_a_log[j],
                           ssd_dt_bias[j], ssd_d[j], ssd_norm[j], odd_w_out, tables)
        h = _ffn(h, norm_ffn[i], ffn_w_gate, ffn_w_up, w_down_bf16, i)
    return _rmsnorm(h, norm_final, out_dtype=x.dtype, row0=LEAD, rows=seq, tr=LANE)
```

```python
import functools
import math

import jax
import jax.numpy as jnp
from jax import lax
from jax.experimental import pallas as pl
from jax.experimental.pallas import tpu as pltpu

F32 = jnp.float32
BF16 = jnp.bfloat16
HIGHEST = lax.Precision.HIGHEST

N_META = 16
LEAD = 128
N_PAD = LEAD - N_META
HEAD_DIM = 128
ROT_DIM = HEAD_DIM // 4
ROPE_THETA = 500000.0
CONV_W = 7
EPS = 1e-6

GDN_HEADS = 16
GDN_DK = 128
GDN_WIDTH = 2048
GDN_CHUNK = 64
GDN_STEP_CHUNKS = 2
DIFF_HEADS = 8
DIFF_DQK = 128
DIFF_DV = 256
DIFF_WIDTH = 2048
SWA_HEADS = 16
SWA_KV_HEADS = 4
SWA_GROUP = 4
SWA_WIDTH = 2048
SWA_WINDOW = 128
SWA_BLOCK = 128
SSD_WIDTH = 2048
SSD_HEADDIM = 64
SSD_HEADS = 32
SSD_STATE = 128
SSD_GROUPS = 4
SSD_GROUP_HEADS = SSD_HEADS // SSD_GROUPS
SSD_GROUP_WIDTH = SSD_WIDTH // SSD_GROUPS
SSD_XBC = 3072
SSD_CHUNK = 128

LANE = 128
NEG = -1e30
LOG2_E = math.log2(math.e)
VMEM_LIMIT = 56 * 1024 * 1024


def _params(*sem):
    return pltpu.CompilerParams(dimension_semantics=sem, vmem_limit_bytes=VMEM_LIMIT)


def _tile(n, pref, align=LANE):
    if n <= pref:
        return n
    t = (pref // align) * align
    while t >= align:
        if n % t == 0:
            return t
        t -= align
    return n


def _dot(a, b, precision=None):
    return jnp.dot(a, b, preferred_element_type=F32, precision=precision)


def _dot_nt(a, b, precision=None):
    return lax.dot_general(a, b, (((1,), (1,)), ((), ())), preferred_element_type=F32, precision=precision)


def _dot_tn(a, b, precision=None):
    return lax.dot_general(a, b, (((0,), (0,)), ((), ())), preferred_element_type=F32, precision=precision)


def _sigmoid(x):
    return 1.0 / (1.0 + jnp.exp(-x))


def _silu(x):
    return x * _sigmoid(x)


def _softplus(x):
    return jnp.maximum(x, 0.0) + jnp.log1p(jnp.exp(-jnp.abs(x)))


def _rmsnorm_kernel(x_ref, w_ref, o_ref, *, eps):
    x = x_ref[...]
    ms = jnp.mean(x * x, axis=-1, keepdims=True)
    o_ref[...] = (x * lax.rsqrt(ms + eps) * w_ref[...]).astype(o_ref.dtype)


def _rmsnorm(h, w, *, out_dtype, row0=0, rows=None, tr):
    bsz, l_len, d = h.shape
    rows = l_len - row0 if rows is None else rows
    off = row0 // tr
    return pl.pallas_call(
        functools.partial(_rmsnorm_kernel, eps=EPS),
        grid=(bsz, rows // tr),
        in_specs=[pl.BlockSpec((None, tr, d), lambda b, i: (b, i + off, 0)),
                  pl.BlockSpec((1, d), lambda b, i: (0, 0))],
        out_specs=pl.BlockSpec((None, tr, d), lambda b, i: (b, i, 0)),
        out_shape=jax.ShapeDtypeStruct((bsz, rows, d), out_dtype),
        compiler_params=_params("parallel", "parallel"),
        name="rmsnorm",
    )(h, w.reshape(1, d))


def _embed_kernel(x_ref, meta_ref, w_ref, h_ref, hw_ref, ssq_ref):
    def emit(blk):
        h_ref[...] = blk
        hw_ref[...] = (blk * w_ref[...]).astype(hw_ref.dtype)
        ssq_ref[...] = jnp.broadcast_to(jnp.sum(blk * blk, axis=-1, keepdims=True), ssq_ref.shape)

    i = pl.program_id(1)

    @pl.when(i == 0)
    def _():
        emit(jnp.concatenate([jnp.zeros((N_PAD, meta_ref.shape[-1]), F32), meta_ref[...]], axis=0))

    @pl.when(i > 0)
    def _():
        emit(x_ref[...])


def _embed_norm(x, meta, w):
    bsz, seq, d = x.shape
    l_pad = LEAD + seq
    blk = pl.BlockSpec((None, LEAD, d), lambda b, i: (b, i, 0))
    return pl.pallas_call(
        _embed_kernel,
        grid=(bsz, l_pad // LEAD),
        in_specs=[pl.BlockSpec((None, LEAD, d), lambda b, i: (b, jnp.maximum(i - 1, 0), 0)),
                  pl.BlockSpec((N_META, d), lambda b, i: (0, 0)),
                  pl.BlockSpec((1, d), lambda b, i: (0, 0))],
        out_specs=[blk, blk, pl.BlockSpec((None, LEAD, LANE), lambda b, i: (b, i, 0))],
        out_shape=[jax.ShapeDtypeStruct((bsz, l_pad, d), F32), jax.ShapeDtypeStruct((bsz, l_pad, d), BF16),
                   jax.ShapeDtypeStruct((bsz, l_pad, LANE), F32)],
        compiler_params=_params("parallel", "parallel"),
        name="embed_norm",
    )(x, meta.astype(F32), w.reshape(1, d))


def _tail_cast_kernel(x_ref, tail_ref, ab_ref, *, ab_w):
    tail_w, tk = tail_ref.shape
    tail_ref[...] = x_ref[ab_w:ab_w + tail_w, :].astype(tail_ref.dtype)
    ab = jnp.concatenate([x_ref[:ab_w, :], jnp.zeros((LANE - ab_w, tk), F32)], axis=0)
    ab_ref[...] = ab.astype(ab_ref.dtype)


def _tail_cast(wt, layer, row0, ab_w, tail_w):
    _, n_total, kdim = wt.shape
    win = row0
    assert row0 + ab_w + tail_w == n_total and ab_w + tail_w <= win
    tk = _tile(kdim, 256)
    return pl.pallas_call(
        functools.partial(_tail_cast_kernel, ab_w=ab_w),
        grid=(kdim // tk,),
        in_specs=[pl.BlockSpec((None, win, tk), lambda k: (layer, 1, k))],
        out_specs=[pl.BlockSpec((None, tail_w, tk), lambda k: (0, 0, k)),
                   pl.BlockSpec((None, LANE, tk), lambda k: (0, 0, k))],
        out_shape=[jax.ShapeDtypeStruct((1, tail_w, kdim), BF16), jax.ShapeDtypeStruct((1, LANE, kdim), BF16)],
        compiler_params=_params("parallel"),
        name="tail_cast",
    )(wt)


def _inv_rms(ssq_ref, d_model):
    return lax.rsqrt(ssq_ref[:, 0:1] * (1.0 / d_model) + EPS)


def _mm_kernel(*refs, trans, has_res, has_scale, zero_pad, emit_norm, tm, d_model):
    n_pairs = len(trans)
    ins = refs[:2 * n_pairs]
    pos = 2 * n_pairs
    res_ref = refs[pos] if has_res else None
    pos += int(has_res)
    ssq_in_ref = refs[pos] if has_scale else None
    pos += int(has_scale)
    nw_ref = refs[pos] if emit_norm else None
    pos += int(emit_norm)
    o_ref = refs[pos]

    acc = None
    for p in range(n_pairs):
        w = ins[2 * p + 1][...].astype(BF16)
        d = _dot_nt(ins[2 * p][...], w) if trans[p] else _dot(ins[2 * p][...], w)
        acc = d if acc is None else acc + d
    if has_scale:
        acc = acc * _inv_rms(ssq_in_ref, d_model)
    if has_res:
        acc = acc + res_ref[...]
    if zero_pad:
        row = pl.program_id(1) * tm + lax.broadcasted_iota(jnp.int32, acc.shape, 0)
        acc = jnp.where(row < N_PAD, 0.0, acc)
    o_ref[...] = acc.astype(o_ref.dtype)
    if emit_norm:
        hw_ref, ssq_ref = refs[pos + 1], refs[pos + 2]
        hw_ref[...] = (acc * nw_ref[...]).astype(hw_ref.dtype)
        part = jnp.broadcast_to(jnp.sum(acc * acc, axis=-1, keepdims=True), ssq_ref.shape)
        j = pl.program_id(2)

        @pl.when(j == 0)
        def _():
            ssq_ref[...] = part

        @pl.when(j > 0)
        def _():
            ssq_ref[...] += part


def _row_spec(block, index_map, single_buffer):
    if single_buffer:
        return pl.BlockSpec(block, index_map, pipeline_mode=pl.Buffered(1))
    return pl.BlockSpec(block, index_map)


def _matmul(pairs, *, out_dtype, tm, tn, n=None, res=None, zero_pad=False, a_single=False, ssq=None, norm_w=None,
            name):
    bsz, l_len, kdim = pairs[0][0].shape
    if n is None:
        n = pairs[0][1].shape[1 if pairs[0][5] else 2]
    in_specs, args = [], []
    for a, w, layer, row0, col0, trans in pairs:
        assert row0 % kdim == 0 and col0 % tn == 0
        r, c = row0 // kdim, col0 // tn
        w_spec = (pl.BlockSpec((None, tn, kdim), lambda b_, i, j, ly=layer, r=r, c=c: (ly, j + c, r)) if trans else
                  pl.BlockSpec((None, kdim, tn), lambda b_, i, j, ly=layer, r=r, c=c: (ly, r, j + c)))
        in_specs += [_row_spec((None, tm, kdim), lambda b_, i, j: (b_, i, 0), a_single), w_spec]
        args += [a, w]
    tile = pl.BlockSpec((None, tm, tn), lambda b_, i, j: (b_, i, j))
    stat = pl.BlockSpec((None, tm, LANE), lambda b_, i, j: (b_, i, 0))
    if res is not None:
        in_specs.append(tile)
        args.append(res)
    if ssq is not None:
        in_specs.append(stat)
        args.append(ssq)
    out_specs, out_shape = tile, jax.ShapeDtypeStruct((bsz, l_len, n), out_dtype)
    if norm_w is not None:
        in_specs.append(pl.BlockSpec((1, tn), lambda b_, i, j: (0, j)))
        args.append(norm_w.reshape(1, n))
        out_specs = [tile, tile, stat]
        out_shape = [out_shape, jax.ShapeDtypeStruct((bsz, l_len, n), BF16),
                     jax.ShapeDtypeStruct((bsz, l_len, LANE), F32)]
    return pl.pallas_call(
        functools.partial(_mm_kernel, trans=tuple(p[5] for p in pairs), has_res=res is not None,
                          has_scale=ssq is not None, zero_pad=zero_pad, emit_norm=norm_w is not None, tm=tm,
                          d_model=kdim),
        grid=(bsz, l_len // tm, n // tn),
        in_specs=in_specs,
        out_specs=out_specs,
        out_shape=out_shape,
        compiler_params=_params("parallel", "parallel", "arbitrary"),
        name=name,
    )(*args)


def _swiglu_kernel(a_ref, g_ref, u_ref, ssq_ref, o_ref):
    a = a_ref[...]
    r = _inv_rms(ssq_ref, a.shape[-1])
    g = _dot(a, g_ref[...].astype(BF16)) * r
    u = _dot(a, u_ref[...].astype(BF16)) * r
    o_ref[...] = (_silu(g) * u).astype(o_ref.dtype)


def _swiglu_up(a, ssq, w_gate, w_up, layer, *, tm, tn):
    bsz, l_len, kdim = a.shape
    n = w_gate.shape[2]
    return pl.pallas_call(
        _swiglu_kernel,
        grid=(bsz, l_len // tm, n // tn),
        in_specs=[_row_spec((None, tm, kdim), lambda b, i, j: (b, i, 0), True),
                  pl.BlockSpec((None, kdim, tn), lambda b, i, j: (layer, 0, j)),
                  pl.BlockSpec((None, kdim, tn), lambda b, i, j: (layer, 0, j)),
                  pl.BlockSpec((None, tm, LANE), lambda b, i, j: (b, i, 0))],
        out_specs=pl.BlockSpec((None, tm, tn), lambda b, i, j: (b, i, j)),
        out_shape=jax.ShapeDtypeStruct((bsz, l_len, n), BF16),
        compiler_params=_params("parallel", "parallel", "parallel"),
        name="swiglu_up",
    )(a, w_gate, w_up, ssq)


def _conv_kernel(*refs, l_len, rc, has_bias):
    if has_bias:
        x_ref, w_ref, b_ref, o_ref, xp_ref = refs
    else:
        x_ref, w_ref, o_ref, xp_ref = refs
        b_ref = None
    tc = x_ref.shape[-1]
    halo = 8
    xp_ref[0:halo, :] = jnp.zeros((halo, tc), F32)
    xp_ref[l_len + halo:l_len + 2 * halo, :] = jnp.zeros((halo, tc), F32)
    xp_ref[halo:l_len + halo, :] = x_ref[...]
    w = w_ref[...]
    reach = CONV_W // 2
    for c in range(l_len // rc):
        r0 = c * rc
        acc = None
        for j in range(CONV_W):
            s = r0 + halo - reach + j
            t = xp_ref[s:s + rc, :] * w[j:j + 1, :]
            acc = t if acc is None else acc + t
        if has_bias:
            acc = acc + b_ref[...]
        y = _silu(acc)
        if r0 < N_PAD:
            row = r0 + lax.broadcasted_iota(jnp.int32, y.shape, 0)
            y = jnp.where(row < N_PAD, 0.0, y)
        o_ref[r0:r0 + rc, :] = y


def _conv_silu(proj, col0, width, w, bias, *, tc=256, rc=128):
    bsz, l_len, _ = proj.shape
    off = col0 // tc
    in_specs = [pl.BlockSpec((None, l_len, tc), lambda b, j: (b, 0, j + off)),
                pl.BlockSpec((CONV_W, tc), lambda b, j: (0, j))]
    args = [proj, w]
    if bias is not None:
        in_specs.append(pl.BlockSpec((1, tc), lambda b, j: (0, j)))
        args.append(bias.reshape(1, width))
    return pl.pallas_call(
        functools.partial(_conv_kernel, l_len=l_len, rc=rc, has_bias=bias is not None),
        grid=(bsz, width // tc),
        in_specs=in_specs,
        out_specs=pl.BlockSpec((None, l_len, tc), lambda b, j: (b, 0, j)),
        out_shape=jax.ShapeDtypeStruct((bsz, l_len, width), F32),
        scratch_shapes=[pltpu.VMEM((l_len + 16, tc), F32)],
        compiler_params=_params("parallel", "parallel"),
        name="conv_silu",
    )(*args)


def _rope(x, cf, s1, s2):
    half = ROT_DIM // 2
    return x * cf + pltpu.roll(x, LANE - half, 1) * s1 + pltpu.roll(x, half, 1) * s2


def _rope_cast_kernel(q_ref, k_ref, v_ref, cf_ref, s1_ref, s2_ref, o_ref, *, q_scale):
    cf, s1, s2 = cf_ref[...], s1_ref[...], s2_ref[...]
    wq, wk, wv = q_ref.shape[-1], k_ref.shape[-1], v_ref.shape[-1]
    for hd in range(wq // HEAD_DIM):
        sl = slice(hd * HEAD_DIM, (hd + 1) * HEAD_DIM)
        o_ref[:, sl] = (_rope(q_ref[:, sl], cf, s1, s2) * q_scale).astype(o_ref.dtype)
    for hd in range(wk // HEAD_DIM):
        sl = slice(hd * HEAD_DIM, (hd + 1) * HEAD_DIM)
        o_ref[:, wq + hd * HEAD_DIM:wq + (hd + 1) * HEAD_DIM] = _rope(k_ref[:, sl], cf, s1, s2).astype(o_ref.dtype)
    o_ref[:, wq + wk:wq + wk + wv] = v_ref[...].astype(o_ref.dtype)


def _rope_cast(proj, cols, tables, *, tr, q_scale):
    bsz, l_len, _ = proj.shape
    width = sum(w for _, w in cols)
    tab_spec = pl.BlockSpec((tr, LANE), lambda b, i: (i, 0))

    def col_spec(col0, w):
        return pl.BlockSpec((None, tr, w), lambda b, i: (b, i, col0 // w))

    return pl.pallas_call(
        functools.partial(_rope_cast_kernel, q_scale=q_scale),
        grid=(bsz, l_len // tr),
        in_specs=[col_spec(*c) for c in cols] + [tab_spec, tab_spec, tab_spec],
        out_specs=pl.BlockSpec((None, tr, width), lambda b, i: (b, i, 0)),
        out_shape=jax.ShapeDtypeStruct((bsz, l_len, width), BF16),
        compiler_params=_params("parallel", "parallel"),
        name="rope_cast",
    )(proj, proj, proj, *tables)


def _rope_tables(l_pad):
    half = ROT_DIM // 2
    inv = ROPE_THETA ** (-jnp.arange(0, ROT_DIM, 2, dtype=F32) / ROT_DIM)
    pos = (jnp.arange(l_pad) - N_PAD).astype(F32)
    ang = pos[:, None] * inv[None, :]
    cos, sin = jnp.cos(ang), jnp.sin(ang)
    ones = jnp.ones((l_pad, LANE - ROT_DIM), F32)
    zeros = jnp.zeros((l_pad, LANE - ROT_DIM), F32)
    zh = jnp.zeros((l_pad, half), F32)
    cf = jnp.concatenate([cos, cos, ones], axis=1)
    s1 = jnp.concatenate([-sin, zh, zeros], axis=1)
    s2 = jnp.concatenate([zh, sin, zeros], axis=1)
    return cf, s1, s2


def _bdot(a, b):
    return _dot(a.astype(BF16), b.astype(BF16))


def _bdot_nt(a, b):
    return _dot_nt(a.astype(BF16), b.astype(BF16))


def _bdot_tn(a, b):
    return _dot_tn(a.astype(BF16), b.astype(BF16))


def _gdn_kernel(*refs, rev, final, nblk):
    if final:
        q_ref, k_ref, v_ref, ab_ref, prm_ref, ob_ref, z_ref, nw_ref, o_ref, s_ref = refs
    else:
        q_ref, k_ref, v_ref, ab_ref, prm_ref, o_ref, s_ref = refs
    c = GDN_CHUNK
    nsub = q_ref.shape[0] // c
    n = pl.program_id(1)

    @pl.when(n == 0)
    def _():
        s_ref[...] = jnp.zeros(s_ref.shape, F32)

    blk = (nblk - 1 - n) if rev else n
    d = 1 if rev else 0
    nh = GDN_HEADS
    heads = range(nh)
    sls = [slice(h * GDN_DK, (h + 1) * GDN_DK) for h in heads]
    cols = [d * nh + h for h in heads]

    ri = lax.broadcasted_iota(jnp.int32, (c, c), 0)
    ci = lax.broadcasted_iota(jnp.int32, (c, c), 1)
    incl = (ri <= ci) if rev else (ri >= ci)
    strict = (ri < ci) if rev else (ri > ci)
    eye = (ri == ci).astype(F32)

    rsl, gcol, tot_h, bcol, egc, qn, kn, kb, decay, vb = [], [], [], [], [], [], [], [], [], []
    for sub in range(nsub):
        rs = slice(sub * c, (sub + 1) * c)
        rows = (blk * nsub + sub) * c + lax.broadcasted_iota(jnp.int32, (c, 1), 0)
        valid = rows >= N_PAD
        ab = ab_ref[rs, :]
        g_all = -jnp.exp(prm_ref[0:1, :]) * _softplus(ab + prm_ref[1:2, :])
        g_all = jnp.where(valid, g_all, 0.0)
        beta_all = jnp.where(valid, _sigmoid(ab), 0.0)
        cum = _dot(incl.astype(F32), g_all, HIGHEST)
        cum_t = _dot_tn(cum, eye, HIGHEST)
        tot = cum[0:1, :] if rev else cum[c - 1:c, :]
        for h in heads:
            col = cols[h]
            g_ = cum[:, col:col + 1]
            b_ = beta_all[:, 2 * nh + col:2 * nh + col + 1]
            qh, kh = q_ref[rs, sls[h]], k_ref[rs, sls[h]]
            kn_ = kh * lax.rsqrt(jnp.sum(kh * kh, axis=-1, keepdims=True) + 1e-6)
            rsl.append(rs)
            gcol.append(g_)
            tot_h.append(tot[:, col:col + 1])
            bcol.append(b_)
            egc.append(jnp.exp(g_))
            qn.append(qh * lax.rsqrt(jnp.sum(qh * qh, axis=-1, keepdims=True) + 1e-6) * (GDN_DK ** -0.5))
            kn.append(kn_)
            kb.append(kn_ * b_)
            vb.append(v_ref[rs, sls[h]] * b_)
            decay.append(jnp.where(incl, jnp.exp(jnp.where(incl, g_ - cum_t[col:col + 1, :], 0.0)), 0.0))
    units = range(nsub * nh)
    kq = [_bdot_nt(jnp.concatenate([kb[u], qn[u]], axis=0), kn[u]) for u in units]
    lower = [jnp.where(strict, kq[u][:c] * decay[u], 0.0) for u in units]
    qk = [kq[u][c:] * decay[u] for u in units]
    x = [eye - lower[u] for u in units]
    p = [_bdot(lower[u], lower[u]) for u in units]
    n_sq = int(math.log2(c)) - 1
    for t in range(n_sq):
        x = [x[u] + _bdot(x[u], p[u]) for u in units]
        if t + 1 < n_sq:
            p = [_bdot(p[u], p[u]) for u in units]
    uw = [_bdot(x[u], jnp.concatenate([vb[u], kb[u] * egc[u]], axis=1)) for u in units]

    s_cur = [s_ref[h] for h in heads]
    for sub in (range(nsub - 1, -1, -1) if rev else range(nsub)):
        us = [sub * nh + h for h in heads]
        wq = [_bdot(jnp.concatenate([uw[u][:, GDN_DK:], qn[u] * egc[u]], axis=0), s_cur[h]) for h, u in enumerate(us)]
        v_new = [uw[u][:, :GDN_DK] - wq[h][:c] for h, u in enumerate(us)]
        o = [wq[h][c:] + _bdot(qk[u], v_new[h]) for h, u in enumerate(us)]
        kv = [_bdot_tn(kn[u] * jnp.exp(tot_h[u] - gcol[u]), v_new[h]) for h, u in enumerate(us)]
        s_cur = [s_cur[h] * jnp.exp(tot_h[u]) + kv[h] for h, u in enumerate(us)]
        for h, u in enumerate(us):
            oh = o[h]
            if final:
                oh = oh + ob_ref[rsl[u], sls[h]]
                oh = oh * lax.rsqrt(jnp.mean(oh * oh, axis=-1, keepdims=True) + EPS) * nw_ref[...]
                oh = oh * _silu(z_ref[rsl[u], sls[h]])
            o_ref[rsl[u], sls[h]] = oh.astype(o_ref.dtype)
    for h in heads:
        s_ref[h] = s_cur[h]


def _gdn_pass(qkv, ab, prm, *, rev, extra=None):
    bsz, l_len, _ = qkv.shape
    c = GDN_STEP_CHUNKS * GDN_CHUNK
    nc = l_len // c
    wd = GDN_WIDTH
    ch = (lambda n: nc - 1 - n) if rev else (lambda n: n)
    in_specs = [pl.BlockSpec((None, c, wd), lambda b, n: (b, ch(n), 0)),
                pl.BlockSpec((None, c, wd), lambda b, n: (b, ch(n), 1)),
                pl.BlockSpec((None, c, wd), lambda b, n: (b, ch(n), 2)),
                pl.BlockSpec((None, c, LANE), lambda b, n: (b, ch(n), 0)),
                pl.BlockSpec((8, LANE), lambda b, n: (0, 0))]
    args = [qkv, qkv, qkv, ab, prm]
    final = extra is not None
    if final:
        o_other, proj, z_col0, norm_w = extra
        zoff = z_col0 // wd
        in_specs += [pl.BlockSpec((None, c, wd), lambda b, n: (b, ch(n), 0)),
                     pl.BlockSpec((None, c, wd), lambda b, n: (b, ch(n), zoff)),
                     pl.BlockSpec((1, GDN_DK), lambda b, n: (0, 0))]
        args += [o_other, proj, norm_w.reshape(1, GDN_DK)]
    return pl.pallas_call(
        functools.partial(_gdn_kernel, rev=rev, final=final, nblk=nc),
        grid=(bsz, nc),
        in_specs=in_specs,
        out_specs=pl.BlockSpec((None, c, wd), lambda b, n: (b, ch(n), 0)),
        out_shape=jax.ShapeDtypeStruct((bsz, l_len, wd), BF16 if final else F32),
        scratch_shapes=[pltpu.VMEM((GDN_HEADS, GDN_DK, GDN_DK), F32)],
        compiler_params=_params("parallel", "arbitrary"),
        name="gdn_bwd" if rev else "gdn_fwd",
    )(*args)


def _diff_kernel(q_ref, k_ref, v_ref, lam_ref, nw_ref, o_ref, *, lam_init):
    key_ok = lax.broadcasted_iota(jnp.int32, (1, LEAD), 1) >= N_PAD
    v = v_ref[...]
    outs = []
    for m in range(2):
        sl = slice(m * DIFF_DQK, (m + 1) * DIFF_DQK)
        s = _dot_nt(q_ref[:, sl], k_ref[:, sl])
        s = jnp.concatenate([jnp.where(key_ok, s[:, :LEAD], NEG), s[:, LEAD:]], axis=1)
        e = jnp.exp2(s - jnp.max(s, axis=-1, keepdims=True))
        den = jnp.sum(e, axis=-1, keepdims=True)
        outs.append(_dot(e.astype(BF16), v) * (1.0 / den))
    lv = lam_ref[...]
    lam = (jnp.exp(jnp.sum(lv[0:1] * lv[1:2], axis=-1, keepdims=True))
           - jnp.exp(jnp.sum(lv[2:3] * lv[3:4], axis=-1, keepdims=True)) + lam_init)
    o = outs[0] - lam * outs[1]
    o = o * lax.rsqrt(jnp.mean(o * o, axis=-1, keepdims=True) + 1e-5) * nw_ref[...]
    o_ref[...] = (o * (1.0 - lam_init)).astype(o_ref.dtype)


def _diff_attention(qkv, lam_vecs, norm_w, lam_init, *, tq):
    bsz, l_len, _ = qkv.shape
    wq = 2 * DIFF_DQK
    nhd = DIFF_HEADS
    return pl.pallas_call(
        functools.partial(_diff_kernel, lam_init=lam_init),
        grid=(bsz, nhd, l_len // tq),
        in_specs=[pl.BlockSpec((None, tq, wq), lambda b, h, i: (b, i, h)),
                  pl.BlockSpec((None, l_len, wq), lambda b, h, i: (b, 0, nhd + h)),
                  pl.BlockSpec((None, l_len, DIFF_DV), lambda b, h, i: (b, 0, 2 * nhd + h)),
                  pl.BlockSpec((4, DIFF_DQK), lambda b, h, i: (0, 0)),
                  pl.BlockSpec((1, DIFF_DV), lambda b, h, i: (0, 0))],
        out_specs=pl.BlockSpec((None, tq, DIFF_DV), lambda b, h, i: (b, i, h)),
        out_shape=jax.ShapeDtypeStruct((bsz, l_len, DIFF_WIDTH), BF16),
        compiler_params=_params("parallel", "parallel", "parallel"),
        name="diff_attention",
    )(qkv, qkv, qkv, lam_vecs, norm_w.reshape(1, DIFF_DV))


def _swa_kernel(sink_ref, q_ref, k_ref, v_ref, o_ref):
    l_len = k_ref.shape[0]
    blk = SWA_BLOCK
    band = 3 * blk
    kvh = pl.program_id(1)
    n = pl.program_id(2)
    start = pl.multiple_of(jnp.clip((n - 1) * blk, 0, l_len - band), blk)
    kb = k_ref[pl.ds(start, band), :]
    vb = v_ref[pl.ds(start, band), :]
    km = k_ref[0:LEAD, :]
    vm = v_ref[0:LEAD, :]
    q = jnp.concatenate([q_ref[:, g * HEAD_DIM:(g + 1) * HEAD_DIM] for g in range(SWA_GROUP)], axis=0)
    rows = SWA_GROUP * blk
    sb = _dot_nt(q, kb)
    sm = _dot_nt(q, km)
    qpos = n * blk + (lax.broadcasted_iota(jnp.int32, (rows, band), 0) & (blk - 1))
    kpos = start + lax.broadcasted_iota(jnp.int32, (rows, band), 1)
    band_ok = (jnp.abs(qpos - kpos) <= SWA_WINDOW) & (kpos >= LEAD)
    sb = jnp.where(band_ok, sb, NEG)
    sm = jnp.where(lax.broadcasted_iota(jnp.int32, (rows, LEAD), 1) >= N_PAD, sm, NEG)
    sk = jnp.concatenate([jnp.full((blk, 1), sink_ref[kvh * SWA_GROUP + g], F32) for g in range(SWA_GROUP)],
                         axis=0)
    mx = jnp.maximum(jnp.maximum(jnp.max(sb, axis=-1, keepdims=True), jnp.max(sm, axis=-1, keepdims=True)), sk)
    eb = jnp.exp(sb - mx)
    em = jnp.exp(sm - mx)
    den = jnp.sum(eb, axis=-1, keepdims=True) + jnp.sum(em, axis=-1, keepdims=True) + jnp.exp(sk - mx)
    o = (_dot(eb.astype(BF16), vb) + _dot(em.astype(BF16), vm)) * (1.0 / den)
    for g in range(SWA_GROUP):
        o_ref[:, g * HEAD_DIM:(g + 1) * HEAD_DIM] = o[g * blk:(g + 1) * blk].astype(o_ref.dtype)


def _window_attention(qkv, sink):
    bsz, l_len, _ = qkv.shape
    wq = SWA_GROUP * HEAD_DIM
    koff = SWA_WIDTH // HEAD_DIM
    voff = koff + SWA_KV_HEADS
    return pl.pallas_call(
        _swa_kernel,
        grid=(bsz, SWA_KV_HEADS, l_len // SWA_BLOCK),
        in_specs=[pl.BlockSpec(memory_space=pltpu.SMEM),
                  pl.BlockSpec((None, SWA_BLOCK, wq), lambda b, h, i: (b, i, h)),
                  pl.BlockSpec((None, l_len, HEAD_DIM), lambda b, h, i: (b, 0, koff + h)),
                  pl.BlockSpec((None, l_len, HEAD_DIM), lambda b, h, i: (b, 0, voff + h))],
        out_specs=pl.BlockSpec((None, SWA_BLOCK, wq), lambda b, h, i: (b, i, h)),
        out_shape=jax.ShapeDtypeStruct((bsz, l_len, SWA_WIDTH), BF16),
        compiler_params=_params("parallel", "parallel", "parallel"),
        name="window_attention",
    )(sink, qkv, qkv, qkv)


def _ssd_kernel(*refs, rev, final, nc):
    if final:
        x_ref, b_ref, c_ref, dt_ref, prm_ref, yb_ref, z0_ref, z1_ref, drow_ref, nw_ref, o_ref, s_ref = refs
    else:
        x_ref, b_ref, c_ref, dt_ref, prm_ref, o_ref, s_ref = refs
    q = SSD_CHUNK
    n = pl.program_id(1)

    @pl.when(n == 0)
    def _():
        s_ref[...] = jnp.zeros(s_ref.shape, F32)

    chunk = (nc - 1 - n) if rev else n
    rows = chunk * q + lax.broadcasted_iota(jnp.int32, (q, 1), 0)
    valid = rows >= N_PAD
    d = 1 if rev else 0
    p = SSD_HEADDIM
    gh = SSD_GROUP_HEADS
    gw = SSD_GROUP_WIDTH

    dt = jnp.where(valid, _softplus(dt_ref[...] + prm_ref[1:2, :]), 0.0)
    da = dt * (-jnp.exp(prm_ref[0:1, :]))
    ri = lax.broadcasted_iota(jnp.int32, (q, q), 0)
    ci = lax.broadcasted_iota(jnp.int32, (q, q), 1)
    incl = (ri <= ci) if rev else (ri >= ci)
    eye = (ri == ci).astype(F32)
    cum = _dot(incl.astype(F32), da, HIGHEST)
    cum_t = _dot_tn(cum, eye, HIGHEST)
    tot = cum[0:1, :] if rev else cum[q - 1:q, :]
    ecum = jnp.exp(cum)
    dstate = jnp.exp(tot - cum)
    etot = jnp.exp(tot)

    lane_lo = lax.broadcasted_iota(jnp.int32, (1, LANE), 1) < p
    groups = range(SSD_GROUPS)
    pairs = range(SSD_HEADS // 2)
    ppg = gh // 2
    col0 = [d * SSD_HEADS + 2 * j for j in pairs]

    def pair_cols(arr, j):
        return jnp.where(lane_lo, arr[:, col0[j]:col0[j] + 1], arr[:, col0[j] + 1:col0[j] + 2])

    def seg(col):
        return jnp.where(incl, jnp.exp(jnp.where(incl, cum[:, col:col + 1] - cum_t[col:col + 1, :], 0.0)), 0.0)

    bg = [b_ref[:, g * SSD_STATE:(g + 1) * SSD_STATE].astype(BF16) for g in groups]
    cg = [c_ref[:, g * SSD_STATE:(g + 1) * SSD_STATE].astype(BF16) for g in groups]
    sg = [s_ref[g * gw:(g + 1) * gw, :] for g in groups]
    cb = [_dot_nt(cg[g], bg[g]) for g in groups]
    y_off = [_dot_nt(cg[g], sg[g].astype(BF16)) for g in groups]
    xdt = [x_ref[:, j * LANE:(j + 1) * LANE] * pair_cols(dt, j) for j in pairs]
    lhs = [jnp.concatenate([cb[j // ppg] * seg(col0[j]), cb[j // ppg] * seg(col0[j] + 1)], axis=1) for j in pairs]
    rhs = [jnp.concatenate([jnp.where(lane_lo, xdt[j], 0.0), jnp.where(lane_lo, 0.0, xdt[j])], axis=0)
           for j in pairs]
    y_diag = [_bdot(lhs[j], rhs[j]) for j in pairs]
    ys = [y_diag[j] + y_off[j // ppg][:, (j % ppg) * LANE:(j % ppg + 1) * LANE] * pair_cols(ecum, j) for j in pairs]
    xs = [xdt[j] * pair_cols(dstate, j) for j in pairs]
    st = [_bdot_tn(jnp.concatenate(xs[g * ppg:(g + 1) * ppg], axis=1), bg[g]) for g in groups]
    for g in groups:
        decs = [jnp.broadcast_to(etot[:, c_:c_ + 1], (p, 1))
                for c_ in range(d * SSD_HEADS + g * gh, d * SSD_HEADS + (g + 1) * gh)]
        s_ref[g * gw:(g + 1) * gw, :] = sg[g] * jnp.concatenate(decs, axis=0) + st[g]
    y = jnp.concatenate(ys, axis=1)
    if final:
        y = y + yb_ref[...] + x_ref[...] * drow_ref[...]
        y = y * _silu(jnp.concatenate([z0_ref[...], z1_ref[...]], axis=1))
        for gi in range(SSD_GROUPS):
            sl = slice(gi * gw, (gi + 1) * gw)
            yg = y[:, sl]
            yg = yg * lax.rsqrt(jnp.mean(yg * yg, axis=-1, keepdims=True) + EPS) * nw_ref[:, sl]
            o_ref[:, sl] = yg.astype(o_ref.dtype)
    else:
        o_ref[...] = y


def _ssd_pass(xbc, dt_raw, prm, *, rev, extra=None):
    bsz, l_len, _ = xbc.shape
    q = SSD_CHUNK
    nc = l_len // q
    wd = SSD_WIDTH
    gs = SSD_GROUPS * SSD_STATE
    ch = (lambda n: nc - 1 - n) if rev else (lambda n: n)
    in_specs = [pl.BlockSpec((None, q, wd), lambda b, n: (b, ch(n), 0)),
                pl.BlockSpec((None, q, gs), lambda b, n: (b, ch(n), wd // gs)),
                pl.BlockSpec((None, q, gs), lambda b, n: (b, ch(n), wd // gs + 1)),
                pl.BlockSpec((None, q, LANE), lambda b, n: (b, ch(n), 0)),
                pl.BlockSpec((8, LANE), lambda b, n: (0, 0))]
    args = [xbc, xbc, xbc, dt_raw, prm]
    final = extra is not None
    if final:
        y_other, proj, z_col0, d_row, norm_w = extra
        zw = wd // 2
        zoff = z_col0 // zw
        in_specs += [pl.BlockSpec((None, q, wd), lambda b, n: (b, ch(n), 0)),
                     pl.BlockSpec((None, q, zw), lambda b, n: (b, ch(n), zoff)),
                     pl.BlockSpec((None, q, zw), lambda b, n: (b, ch(n), zoff + 1)),
                     pl.BlockSpec((1, wd), lambda b, n: (0, 0)),
                     pl.BlockSpec((1, wd), lambda b, n: (0, 0))]
        args += [y_other, proj, proj, d_row, norm_w.reshape(1, wd)]
    return pl.pallas_call(
        functools.partial(_ssd_kernel, rev=rev, final=final, nc=nc),
        grid=(bsz, nc),
        in_specs=in_specs,
        out_specs=pl.BlockSpec((None, q, wd), lambda b, n: (b, ch(n), 0)),
        out_shape=jax.ShapeDtypeStruct((bsz, l_len, wd), BF16 if final else F32),
        scratch_shapes=[pltpu.VMEM((wd, SSD_STATE), F32)],
        compiler_params=_params("parallel", "arbitrary"),
        name="ssd_bwd" if rev else "ssd_fwd",
    )(*args)


def _lane_params(*rows):
    out = jnp.zeros((8, LANE), F32)
    for i, r in enumerate(rows):
        r = r.astype(F32).reshape(-1)
        out = out.at[i, :r.shape[0]].set(r)
    return out


def _pad_cols(w, n):
    return jnp.pad(w, ((0, 0), (0, n - w.shape[1])))


def _diff_lambda_init(layer):
    return 0.8 - 0.6 * math.exp(-0.3 * layer)


def _tiles(l_pad):
    return _tile(l_pad, 1056, 16)


def _tall_tile(l_pad):
    return _tile(l_pad, 2112, 16)


def _even_mixer(h, hw, ssq, w_in, j, conv_w, a_log, dt_bias, gdn_norm_w, lam_vecs, diff_norm_w, w_out, lam_init,
                tables, next_norm_w):
    bsz, l_pad, d = h.shape
    tm = _tiles(l_pad)
    qkv_w = 3 * GDN_WIDTH
    main_w = qkv_w + GDN_WIDTH
    ab_w = 4 * GDN_HEADS
    wt = jnp.swapaxes(w_in, 1, 2)
    w_tail, w_ab = _tail_cast(wt, j, main_w, ab_w, wt.shape[1] - main_w - ab_w)
    tall = _tall_tile(l_pad)
    proj_a = _matmul([(hw, wt, j, 0, 0, True)], n=main_w, out_dtype=F32, tm=tall, tn=_tile(main_w, 512),
                     a_single=True, ssq=ssq, name="even_in_proj_a")
    proj_b = _matmul([(hw, w_tail, 0, 0, 0, True)], out_dtype=F32, tm=tall, tn=_tile(w_tail.shape[1], 512),
                     a_single=True, ssq=ssq, name="even_in_proj_b")
    ab = _matmul([(hw, w_ab, 0, 0, 0, True)], out_dtype=F32, tm=tm, tn=LANE, ssq=ssq, name="even_ab_proj")

    qkv_a = _conv_silu(proj_a, 0, qkv_w, conv_w, None)
    prm = _lane_params(a_log, dt_bias)
    o_bwd = _gdn_pass(qkv_a, ab, prm, rev=True)
    o_a = _gdn_pass(qkv_a, ab, prm, rev=False, extra=(o_bwd, proj_a, qkv_w, gdn_norm_w))

    qk_w = 2 * DIFF_HEADS * DIFF_DQK
    cols = ((0, qk_w), (qk_w, qk_w), (2 * qk_w, DIFF_WIDTH))
    qkv_b = _rope_cast(proj_b, cols, tables, tr=_tile(l_pad, 384, 16), q_scale=DIFF_DQK ** -0.5 * LOG2_E)
    o_b = _diff_attention(qkv_b, lam_vecs, diff_norm_w, lam_init, tq=_tile(l_pad, 384, 16))

    return _matmul([(o_a, w_out, j, 0, 0, False), (o_b, w_out, j, GDN_WIDTH, 0, False)], out_dtype=F32, tm=tm,
                   tn=_tile(d, 512), res=h, zero_pad=True, norm_w=next_norm_w, name="even_out_proj")


def _odd_mixer(h, hw, ssq, w_in, j, sink, conv_w, conv_b, a_log, dt_bias, d_skip, ssd_norm_w, w_out, tables,
               next_norm_w):
    bsz, l_pad, d = h.shape
    tm = _tiles(l_pad)
    kv_w = SWA_KV_HEADS * HEAD_DIM
    z_col = SWA_WIDTH + 2 * kv_w
    xbc_col = z_col + SSD_WIDTH
    main_w = xbc_col + SSD_XBC
    wt = jnp.swapaxes(w_in, 1, 2)
    w_dt = jnp.pad(wt[j, main_w:], ((0, LANE - (wt.shape[1] - main_w)), (0, 0))).astype(BF16)[None]
    proj = _matmul([(hw, wt, j, 0, 0, True)], n=main_w, out_dtype=F32, tm=_tall_tile(l_pad),
                   tn=_tile(main_w, 512), a_single=True, ssq=ssq, name="odd_in_proj")
    dt_raw = _matmul([(hw, w_dt, 0, 0, 0, True)], out_dtype=F32, tm=tm, tn=LANE, ssq=ssq, name="odd_dt_proj")

    cols = ((0, SWA_WIDTH), (SWA_WIDTH, kv_w), (SWA_WIDTH + kv_w, kv_w))
    qkv_c = _rope_cast(proj, cols, tables, tr=_tile(l_pad, 384, 16), q_scale=HEAD_DIM ** -0.5)
    o_c = _window_attention(qkv_c, sink.astype(F32))

    xbc = _conv_silu(proj, xbc_col, SSD_XBC, conv_w, conv_b)
    prm = _lane_params(a_log, dt_bias)
    d_row = jnp.repeat(d_skip.astype(F32), SSD_HEADDIM).reshape(1, SSD_WIDTH)
    y_bwd = _ssd_pass(xbc, dt_raw, prm, rev=True)
    y = _ssd_pass(xbc, dt_raw, prm, rev=False, extra=(y_bwd, proj, z_col, d_row, ssd_norm_w))

    return _matmul([(o_c, w_out, j, 0, 0, False), (y, w_out, j, SWA_WIDTH, 0, False)], out_dtype=F32, tm=tm,
                   tn=_tile(d, 512), res=h, zero_pad=True, norm_w=next_norm_w, name="odd_out_proj")


def _ffn(h, hw, ssq, w_gate, w_up, w_down_bf16, layer, next_norm_w):
    bsz, l_pad, d = h.shape
    f = w_gate.shape[2]
    act = _swiglu_up(hw, ssq, w_gate, w_up, layer, tm=_tall_tile(l_pad), tn=_tile(f, 256))
    return _matmul([(act, w_down_bf16, layer, 0, 0, False)], out_dtype=F32, tm=_tile(l_pad, 704, 16),
                   tn=_tile(d, 256), res=h, norm_w=next_norm_w, name="ffn_down")


def kernel(x, meta_tokens, norm_mix, norm_ffn, norm_final, even_w_in, even_conv, gdn_a_log, gdn_dt_bias, gdn_norm, diff_lam_q1, diff_lam_k1, diff_lam_q2, diff_lam_k2, diff_norm, even_w_out, odd_w_in, swa_sink, ssd_conv_w, ssd_conv_b, ssd_a_log, ssd_dt_bias, ssd_d, ssd_norm, odd_w_out, ffn_w_gate, ffn_w_up, ffn_w_down):
    bsz, seq, d = x.shape
    depth = norm_mix.shape[0]
    l_pad = LEAD + seq
    tables = _rope_tables(l_pad)
    w_down_bf16 = ffn_w_down.astype(BF16)
    h, hw, ssq = _embed_norm(x, meta_tokens, norm_mix[0])
    for i in range(depth):
        j = i // 2
        if i % 2 == 0:
            lam_vecs = jnp.stack([diff_lam_q1[j], diff_lam_k1[j], diff_lam_q2[j], diff_lam_k2[j]]).astype(F32)
            h, hw, ssq = _even_mixer(h, hw, ssq, even_w_in, j, even_conv[j], gdn_a_log[j], gdn_dt_bias[j],
                                     gdn_norm[j], lam_vecs, diff_norm[j], even_w_out, _diff_lambda_init(i), tables,
                                     norm_ffn[i])
        else:
            h, hw, ssq = _odd_mixer(h, hw, ssq, odd_w_in, j, swa_sink[j], ssd_conv_w[j], ssd_conv_b[j], ssd_a_log[j],
                                    ssd_dt_bias[j], ssd_d[j], ssd_norm[j], odd_w_out, tables, norm_ffn[i])
        if i + 1 < depth:
            h, hw, ssq = _ffn(h, hw, ssq, ffn_w_gate, ffn_w_up, w_down_bf16, i, norm_mix[i + 1])
        else:
            h = _ffn(h, hw, ssq, ffn_w_gate, ffn_w_up, w_down_bf16, i, None)
    return _rmsnorm(h, norm_final, out_dtype=x.dtype, row0=LEAD, rows=seq, tr=LANE)
```

```python
import functools
import math

import jax
import jax.numpy as jnp
from jax import lax
from jax.experimental import pallas as pl
from jax.experimental.pallas import tpu as pltpu

F32 = jnp.float32
BF16 = jnp.bfloat16
HIGHEST = lax.Precision.HIGHEST

N_META = 16
LEAD = 128
N_PAD = LEAD - N_META
HEAD_DIM = 128
ROT_DIM = HEAD_DIM // 4
ROPE_THETA = 500000.0
CONV_W = 7
EPS = 1e-6

GDN_HEADS = 16
GDN_DK = 128
GDN_WIDTH = 2048
GDN_CHUNK = 64
GDN_STEP_CHUNKS = 2
GDN_GROUP_HEADS = 4
DIFF_HEADS = 8
DIFF_DQK = 128
DIFF_DV = 256
DIFF_WIDTH = 2048
SWA_HEADS = 16
SWA_KV_HEADS = 4
SWA_GROUP = 4
SWA_WIDTH = 2048
SWA_WINDOW = 128
SWA_BLOCK = 128
SSD_WIDTH = 2048
SSD_HEADDIM = 64
SSD_HEADS = 32
SSD_STATE = 128
SSD_GROUPS = 4
SSD_GROUP_HEADS = SSD_HEADS // SSD_GROUPS
SSD_GROUP_WIDTH = SSD_WIDTH // SSD_GROUPS
SSD_XBC = 3072
SSD_CHUNK = 128

LANE = 128
NEG = -1e30
LOG2_E = math.log2(math.e)
VMEM_LIMIT = 56 * 1024 * 1024


def _params(*sem):
    return pltpu.CompilerParams(dimension_semantics=sem, vmem_limit_bytes=VMEM_LIMIT)


def _tile(n, pref, align=LANE):
    if n <= pref:
        return n
    t = (pref // align) * align
    while t >= align:
        if n % t == 0:
            return t
        t -= align
    return n


def _dot(a, b, precision=None):
    return jnp.dot(a, b, preferred_element_type=F32, precision=precision)


def _dot_nt(a, b, precision=None):
    return lax.dot_general(a, b, (((1,), (1,)), ((), ())), preferred_element_type=F32, precision=precision)


def _dot_tn(a, b, precision=None):
    return lax.dot_general(a, b, (((0,), (0,)), ((), ())), preferred_element_type=F32, precision=precision)


def _sigmoid(x):
    return 1.0 / (1.0 + jnp.exp(-x))


def _silu(x):
    return x * _sigmoid(x)


def _softplus(x):
    return jnp.maximum(x, 0.0) + jnp.log1p(jnp.exp(-jnp.abs(x)))


def _rmsnorm_kernel(x_ref, w_ref, o_ref, *, eps):
    x = x_ref[...]
    ms = jnp.mean(x * x, axis=-1, keepdims=True)
    o_ref[...] = (x * lax.rsqrt(ms + eps) * w_ref[...]).astype(o_ref.dtype)


def _rmsnorm(h, w, *, out_dtype, row0=0, rows=None, tr):
    bsz, l_len, d = h.shape
    rows = l_len - row0 if rows is None else rows
    off = row0 // tr
    return pl.pallas_call(
        functools.partial(_rmsnorm_kernel, eps=EPS),
        grid=(bsz, rows // tr),
        in_specs=[pl.BlockSpec((None, tr, d), lambda b, i: (b, i + off, 0)),
                  pl.BlockSpec((1, d), lambda b, i: (0, 0))],
        out_specs=pl.BlockSpec((None, tr, d), lambda b, i: (b, i, 0)),
        out_shape=jax.ShapeDtypeStruct((bsz, rows, d), out_dtype),
        compiler_params=_params("parallel", "parallel"),
        name="rmsnorm",
    )(h, w.reshape(1, d))


def _embed_kernel(x_ref, meta_ref, w_ref, h_ref, hw_ref, ssq_ref):
    def emit(blk):
        h_ref[...] = blk
        hw_ref[...] = (blk * w_ref[...]).astype(hw_ref.dtype)
        ssq_ref[...] = jnp.broadcast_to(jnp.sum(blk * blk, axis=-1, keepdims=True), ssq_ref.shape)

    i = pl.program_id(1)

    @pl.when(i == 0)
    def _():
        emit(jnp.concatenate([jnp.zeros((N_PAD, meta_ref.shape[-1]), F32), meta_ref[...]], axis=0))

    @pl.when(i > 0)
    def _():
        emit(x_ref[...])


def _embed_norm(x, meta, w):
    bsz, seq, d = x.shape
    l_pad = LEAD + seq
    blk = pl.BlockSpec((None, LEAD, d), lambda b, i: (b, i, 0))
    return pl.pallas_call(
        _embed_kernel,
        grid=(bsz, l_pad // LEAD),
        in_specs=[pl.BlockSpec((None, LEAD, d), lambda b, i: (b, jnp.maximum(i - 1, 0), 0)),
                  pl.BlockSpec((N_META, d), lambda b, i: (0, 0)),
                  pl.BlockSpec((1, d), lambda b, i: (0, 0))],
        out_specs=[blk, blk, pl.BlockSpec((None, LEAD, LANE), lambda b, i: (b, i, 0))],
        out_shape=[jax.ShapeDtypeStruct((bsz, l_pad, d), F32), jax.ShapeDtypeStruct((bsz, l_pad, d), BF16),
                   jax.ShapeDtypeStruct((bsz, l_pad, LANE), F32)],
        compiler_params=_params("parallel", "parallel"),
        name="embed_norm",
    )(x, meta.astype(F32), w.reshape(1, d))


def _tail_cast_kernel(x_ref, tail_ref, ab_ref, *, ab_w):
    tail_w, tk = tail_ref.shape
    tail_ref[...] = x_ref[ab_w:ab_w + tail_w, :].astype(tail_ref.dtype)
    ab = jnp.concatenate([x_ref[:ab_w, :], jnp.zeros((LANE - ab_w, tk), F32)], axis=0)
    ab_ref[...] = ab.astype(ab_ref.dtype)


def _tail_cast(wt, layer, row0, ab_w, tail_w):
    _, n_total, kdim = wt.shape
    win = row0
    assert row0 + ab_w + tail_w == n_total and ab_w + tail_w <= win
    tk = _tile(kdim, 256)
    return pl.pallas_call(
        functools.partial(_tail_cast_kernel, ab_w=ab_w),
        grid=(kdim // tk,),
        in_specs=[pl.BlockSpec((None, win, tk), lambda k: (layer, 1, k))],
        out_specs=[pl.BlockSpec((None, tail_w, tk), lambda k: (0, 0, k)),
                   pl.BlockSpec((None, LANE, tk), lambda k: (0, 0, k))],
        out_shape=[jax.ShapeDtypeStruct((1, tail_w, kdim), BF16), jax.ShapeDtypeStruct((1, LANE, kdim), BF16)],
        compiler_params=_params("parallel"),
        name="tail_cast",
    )(wt)


def _inv_rms(ssq_ref, d_model):
    return lax.rsqrt(ssq_ref[:, 0:1] * (1.0 / d_model) + EPS)


def _mm_kernel(*refs, trans, has_res, has_scale, zero_pad, emit_norm, tm, d_model):
    n_pairs = len(trans)
    ins = refs[:2 * n_pairs]
    pos = 2 * n_pairs
    res_ref = refs[pos] if has_res else None
    pos += int(has_res)
    ssq_in_ref = refs[pos] if has_scale else None
    pos += int(has_scale)
    nw_ref = refs[pos] if emit_norm else None
    pos += int(emit_norm)
    o_ref = refs[pos]

    acc = None
    for p in range(n_pairs):
        w = ins[2 * p + 1][...].astype(BF16)
        d = _dot_nt(ins[2 * p][...], w) if trans[p] else _dot(ins[2 * p][...], w)
        acc = d if acc is None else acc + d
    if has_scale:
        acc = acc * _inv_rms(ssq_in_ref, d_model)
    if has_res:
        acc = acc + res_ref[...]
    if zero_pad:
        row = pl.program_id(1) * tm + lax.broadcasted_iota(jnp.int32, acc.shape, 0)
        acc = jnp.where(row < N_PAD, 0.0, acc)
    o_ref[...] = acc.astype(o_ref.dtype)
    if emit_norm:
        hw_ref, ssq_ref = refs[pos + 1], refs[pos + 2]
        hw_ref[...] = (acc * nw_ref[...]).astype(hw_ref.dtype)
        part = jnp.broadcast_to(jnp.sum(acc * acc, axis=-1, keepdims=True), ssq_ref.shape)
        j = pl.program_id(2)

        @pl.when(j == 0)
        def _():
            ssq_ref[...] = part

        @pl.when(j > 0)
        def _():
            ssq_ref[...] += part


def _row_spec(block, index_map, single_buffer):
    if single_buffer:
        return pl.BlockSpec(block, index_map, pipeline_mode=pl.Buffered(1))
    return pl.BlockSpec(block, index_map)


def _matmul(pairs, *, out_dtype, tm, tn, n=None, res=None, zero_pad=False, a_single=False, ssq=None, norm_w=None,
            name):
    bsz, l_len, kdim = pairs[0][0].shape
    if n is None:
        n = pairs[0][1].shape[1 if pairs[0][5] else 2]
    in_specs, args = [], []
    for a, w, layer, row0, col0, trans in pairs:
        assert row0 % kdim == 0 and col0 % tn == 0
        r, c = row0 // kdim, col0 // tn
        w_spec = (pl.BlockSpec((None, tn, kdim), lambda b_, i, j, ly=layer, r=r, c=c: (ly, j + c, r)) if trans else
                  pl.BlockSpec((None, kdim, tn), lambda b_, i, j, ly=layer, r=r, c=c: (ly, r, j + c)))
        in_specs += [_row_spec((None, tm, kdim), lambda b_, i, j: (b_, i, 0), a_single), w_spec]
        args += [a, w]
    tile = pl.BlockSpec((None, tm, tn), lambda b_, i, j: (b_, i, j))
    stat = pl.BlockSpec((None, tm, LANE), lambda b_, i, j: (b_, i, 0))
    if res is not None:
        in_specs.append(tile)
        args.append(res)
    if ssq is not None:
        in_specs.append(stat)
        args.append(ssq)
    out_specs, out_shape = tile, jax.ShapeDtypeStruct((bsz, l_len, n), out_dtype)
    if norm_w is not None:
        in_specs.append(pl.BlockSpec((1, tn), lambda b_, i, j: (0, j)))
        args.append(norm_w.reshape(1, n))
        out_specs = [tile, tile, stat]
        out_shape = [out_shape, jax.ShapeDtypeStruct((bsz, l_len, n), BF16),
                     jax.ShapeDtypeStruct((bsz, l_len, LANE), F32)]
    return pl.pallas_call(
        functools.partial(_mm_kernel, trans=tuple(p[5] for p in pairs), has_res=res is not None,
                          has_scale=ssq is not None, zero_pad=zero_pad, emit_norm=norm_w is not None, tm=tm,
                          d_model=kdim),
        grid=(bsz, l_len // tm, n // tn),
        in_specs=in_specs,
        out_specs=out_specs,
        out_shape=out_shape,
        compiler_params=_params("parallel", "parallel", "arbitrary"),
        name=name,
    )(*args)


def _swiglu_kernel(a_ref, g_ref, u_ref, ssq_ref, o_ref):
    a = a_ref[...]
    r = _inv_rms(ssq_ref, a.shape[-1])
    g = _dot(a, g_ref[...].astype(BF16)) * r
    u = _dot(a, u_ref[...].astype(BF16)) * r
    o_ref[...] = (_silu(g) * u).astype(o_ref.dtype)


def _swiglu_up(a, ssq, w_gate, w_up, layer, *, tm, tn):
    bsz, l_len, kdim = a.shape
    n = w_gate.shape[2]
    return pl.pallas_call(
        _swiglu_kernel,
        grid=(bsz, l_len // tm, n // tn),
        in_specs=[_row_spec((None, tm, kdim), lambda b, i, j: (b, i, 0), True),
                  pl.BlockSpec((None, kdim, tn), lambda b, i, j: (layer, 0, j)),
                  pl.BlockSpec((None, kdim, tn), lambda b, i, j: (layer, 0, j)),
                  pl.BlockSpec((None, tm, LANE), lambda b, i, j: (b, i, 0))],
        out_specs=pl.BlockSpec((None, tm, tn), lambda b, i, j: (b, i, j)),
        out_shape=jax.ShapeDtypeStruct((bsz, l_len, n), BF16),
        compiler_params=_params("parallel", "parallel", "parallel"),
        name="swiglu_up",
    )(a, w_gate, w_up, ssq)


def _conv_kernel(*refs, l_len, rc, has_bias):
    if has_bias:
        x_ref, w_ref, b_ref, o_ref, xp_ref = refs
    else:
        x_ref, w_ref, o_ref, xp_ref = refs
        b_ref = None
    tc = x_ref.shape[-1]
    halo = 8
    xp_ref[0:halo, :] = jnp.zeros((halo, tc), F32)
    xp_ref[l_len + halo:l_len + 2 * halo, :] = jnp.zeros((halo, tc), F32)
    xp_ref[halo:l_len + halo, :] = x_ref[...]
    w = w_ref[...]
    reach = CONV_W // 2
    for c in range(l_len // rc):
        r0 = c * rc
        acc = None
        for j in range(CONV_W):
            s = r0 + halo - reach + j
            t = xp_ref[s:s + rc, :] * w[j:j + 1, :]
            acc = t if acc is None else acc + t
        if has_bias:
            acc = acc + b_ref[...]
        y = _silu(acc)
        if r0 < N_PAD:
            row = r0 + lax.broadcasted_iota(jnp.int32, y.shape, 0)
            y = jnp.where(row < N_PAD, 0.0, y)
        o_ref[r0:r0 + rc, :] = y


def _conv_silu(proj, col0, width, w, bias, *, tc=256, rc=128):
    bsz, l_len, _ = proj.shape
    off = col0 // tc
    in_specs = [pl.BlockSpec((None, l_len, tc), lambda b, j: (b, 0, j + off)),
                pl.BlockSpec((CONV_W, tc), lambda b, j: (0, j))]
    args = [proj, w]
    if bias is not None:
        in_specs.append(pl.BlockSpec((1, tc), lambda b, j: (0, j)))
        args.append(bias.reshape(1, width))
    return pl.pallas_call(
        functools.partial(_conv_kernel, l_len=l_len, rc=rc, has_bias=bias is not None),
        grid=(bsz, width // tc),
        in_specs=in_specs,
        out_specs=pl.BlockSpec((None, l_len, tc), lambda b, j: (b, 0, j)),
        out_shape=jax.ShapeDtypeStruct((bsz, l_len, width), F32),
        scratch_shapes=[pltpu.VMEM((l_len + 16, tc), F32)],
        compiler_params=_params("parallel", "parallel"),
        name="conv_silu",
    )(*args)


def _rope(x, cf, s1, s2):
    half = ROT_DIM // 2
    return x * cf + pltpu.roll(x, LANE - half, 1) * s1 + pltpu.roll(x, half, 1) * s2


def _rope_cast_kernel(q_ref, k_ref, v_ref, cf_ref, s1_ref, s2_ref, o_ref, *, q_scale):
    cf, s1, s2 = cf_ref[...], s1_ref[...], s2_ref[...]
    wq, wk, wv = q_ref.shape[-1], k_ref.shape[-1], v_ref.shape[-1]
    for hd in range(wq // HEAD_DIM):
        sl = slice(hd * HEAD_DIM, (hd + 1) * HEAD_DIM)
        o_ref[:, sl] = (_rope(q_ref[:, sl], cf, s1, s2) * q_scale).astype(o_ref.dtype)
    for hd in range(wk // HEAD_DIM):
        sl = slice(hd * HEAD_DIM, (hd + 1) * HEAD_DIM)
        o_ref[:, wq + hd * HEAD_DIM:wq + (hd + 1) * HEAD_DIM] = _rope(k_ref[:, sl], cf, s1, s2).astype(o_ref.dtype)
    o_ref[:, wq + wk:wq + wk + wv] = v_ref[...].astype(o_ref.dtype)


def _rope_cast(proj, cols, tables, *, tr, q_scale):
    bsz, l_len, _ = proj.shape
    width = sum(w for _, w in cols)
    tab_spec = pl.BlockSpec((tr, LANE), lambda b, i: (i, 0))

    def col_spec(col0, w):
        return pl.BlockSpec((None, tr, w), lambda b, i: (b, i, col0 // w))

    return pl.pallas_call(
        functools.partial(_rope_cast_kernel, q_scale=q_scale),
        grid=(bsz, l_len // tr),
        in_specs=[col_spec(*c) for c in cols] + [tab_spec, tab_spec, tab_spec],
        out_specs=pl.BlockSpec((None, tr, width), lambda b, i: (b, i, 0)),
        out_shape=jax.ShapeDtypeStruct((bsz, l_len, width), BF16),
        compiler_params=_params("parallel", "parallel"),
        name="rope_cast",
    )(proj, proj, proj, *tables)


def _rope_tables(l_pad):
    half = ROT_DIM // 2
    inv = ROPE_THETA ** (-jnp.arange(0, ROT_DIM, 2, dtype=F32) / ROT_DIM)
    pos = (jnp.arange(l_pad) - N_PAD).astype(F32)
    ang = pos[:, None] * inv[None, :]
    cos, sin = jnp.cos(ang), jnp.sin(ang)
    ones = jnp.ones((l_pad, LANE - ROT_DIM), F32)
    zeros = jnp.zeros((l_pad, LANE - ROT_DIM), F32)
    zh = jnp.zeros((l_pad, half), F32)
    cf = jnp.concatenate([cos, cos, ones], axis=1)
    s1 = jnp.concatenate([-sin, zh, zeros], axis=1)
    s2 = jnp.concatenate([zh, sin, zeros], axis=1)
    return cf, s1, s2


def _bdot(a, b):
    return _dot(a.astype(BF16), b.astype(BF16))


def _bdot_nt(a, b):
    return _dot_nt(a.astype(BF16), b.astype(BF16))


def _bdot_tn(a, b):
    return _dot_tn(a.astype(BF16), b.astype(BF16))


def _gdn_kernel(*refs, rev, final, nblk):
    if final:
        q_ref, k_ref, v_ref, ab_ref, prm_ref, ob_ref, z_ref, nw_ref, o_ref, s_ref = refs
    else:
        q_ref, k_ref, v_ref, ab_ref, prm_ref, o_ref, s_ref = refs
    c = GDN_CHUNK
    nsub = q_ref.shape[0] // c
    n = pl.program_id(1)

    @pl.when(n == 0)
    def _():
        s_ref[...] = jnp.zeros(s_ref.shape, F32)

    blk = (nblk - 1 - n) if rev else n
    d = 1 if rev else 0
    nh = GDN_HEADS
    heads = range(nh)
    sls = [slice(h * GDN_DK, (h + 1) * GDN_DK) for h in heads]
    cols = [d * nh + h for h in heads]

    ri = lax.broadcasted_iota(jnp.int32, (c, c), 0)
    ci = lax.broadcasted_iota(jnp.int32, (c, c), 1)
    incl = (ri <= ci) if rev else (ri >= ci)
    strict = (ri < ci) if rev else (ri > ci)
    eye = (ri == ci).astype(F32)

    rsl, cums, cum_ts, tots, betas = [], [], [], [], []
    for sub in range(nsub):
        rs = slice(sub * c, (sub + 1) * c)
        rows = (blk * nsub + sub) * c + lax.broadcasted_iota(jnp.int32, (c, 1), 0)
        valid = rows >= N_PAD
        ab = ab_ref[rs, :]
        g_all = -jnp.exp(prm_ref[0:1, :]) * _softplus(ab + prm_ref[1:2, :])
        g_all = jnp.where(valid, g_all, 0.0)
        cum = _dot(incl.astype(F32), g_all, HIGHEST)
        rsl.append(rs)
        cums.append(cum)
        cum_ts.append(_dot_tn(cum, eye, HIGHEST))
        tots.append(cum[0:1, :] if rev else cum[c - 1:c, :])
        betas.append(jnp.where(valid, _sigmoid(ab), 0.0))
    n_sq = int(math.log2(c)) - 1
    scan_order = range(nsub - 1, -1, -1) if rev else range(nsub)

    def head_group(hs):
        units = [(sub, h) for sub in range(nsub) for h in hs]
        ids = range(len(units))
        gcol = [cums[sub][:, cols[h]:cols[h] + 1] for sub, h in units]
        tot_h = [tots[sub][:, cols[h]:cols[h] + 1] for sub, h in units]
        bcol = [betas[sub][:, 2 * nh + cols[h]:2 * nh + cols[h] + 1] for sub, h in units]
        egc = [jnp.exp(g_) for g_ in gcol]
        qn, kn = [], []
        for sub, h in units:
            qh, kh = q_ref[rsl[sub], sls[h]], k_ref[rsl[sub], sls[h]]
            qn.append(qh * lax.rsqrt(jnp.sum(qh * qh, axis=-1, keepdims=True) + 1e-6) * (GDN_DK ** -0.5))
            kn.append(kh * lax.rsqrt(jnp.sum(kh * kh, axis=-1, keepdims=True) + 1e-6))
        kb = [kn[u] * bcol[u] for u in ids]
        vb = [v_ref[rsl[sub], sls[h]] * bcol[u] for u, (sub, h) in enumerate(units)]
        decay = [jnp.where(incl, jnp.exp(jnp.where(incl, gcol[u] - cum_ts[sub][cols[h]:cols[h] + 1, :], 0.0)), 0.0)
                 for u, (sub, h) in enumerate(units)]
        yield
        kq = [_bdot_nt(jnp.concatenate([kb[u], qn[u]], axis=0), kn[u]) for u in ids]
        yield
        lower = [jnp.where(strict, kq[u][:c] * decay[u], 0.0) for u in ids]
        qk = [kq[u][c:] * decay[u] for u in ids]
        x = [eye - lower[u] for u in ids]
        p = [_bdot(lower[u], lower[u]) for u in ids]
        yield
        for t in range(n_sq):
            x = [x[u] + _bdot(x[u], p[u]) for u in ids]
            if t + 1 < n_sq:
                p = [_bdot(p[u], p[u]) for u in ids]
            yield
        uw = [_bdot(x[u], jnp.concatenate([vb[u], kb[u] * egc[u]], axis=1)) for u in ids]
        yield
        s_cur = [s_ref[h] for h in hs]
        for sub in scan_order:
            us = [sub * len(hs) + i for i in range(len(hs))]
            wq = [_bdot(jnp.concatenate([uw[u][:, GDN_DK:], qn[u] * egc[u]], axis=0), s_cur[i])
                  for i, u in enumerate(us)]
            yield
            v_new = [uw[u][:, :GDN_DK] - wq[i][:c] for i, u in enumerate(us)]
            o = [wq[i][c:] + _bdot(qk[u], v_new[i]) for i, u in enumerate(us)]
            kv = [_bdot_tn(kn[u] * jnp.exp(tot_h[u] - gcol[u]), v_new[i]) for i, u in enumerate(us)]
            yield
            s_cur = [s_cur[i] * jnp.exp(tot_h[u]) + kv[i] for i, u in enumerate(us)]
            for i, h in enumerate(hs):
                oh = o[i]
                if final:
                    oh = oh + ob_ref[rsl[sub], sls[h]]
                    oh = oh * lax.rsqrt(jnp.mean(oh * oh, axis=-1, keepdims=True) + EPS) * nw_ref[...]
                    oh = oh * _silu(z_ref[rsl[sub], sls[h]])
                o_ref[rsl[sub], sls[h]] = oh.astype(o_ref.dtype)
        for i, h in enumerate(hs):
            s_ref[h] = s_cur[i]

    groups = [head_group(list(range(g, g + GDN_GROUP_HEADS))) for g in range(0, nh, GDN_GROUP_HEADS)]
    live = [True] * len(groups)
    rnd = 0
    while any(live):
        for g, gen in enumerate(groups):
            if live[g] and rnd >= g:
                live[g] = next(gen, "done") != "done"
        rnd += 1


def _gdn_pass(qkv, ab, prm, *, rev, extra=None):
    bsz, l_len, _ = qkv.shape
    c = GDN_STEP_CHUNKS * GDN_CHUNK
    nc = l_len // c
    wd = GDN_WIDTH
    ch = (lambda n: nc - 1 - n) if rev else (lambda n: n)
    in_specs = [pl.BlockSpec((None, c, wd), lambda b, n: (b, ch(n), 0)),
                pl.BlockSpec((None, c, wd), lambda b, n: (b, ch(n), 1)),
                pl.BlockSpec((None, c, wd), lambda b, n: (b, ch(n), 2)),
                pl.BlockSpec((None, c, LANE), lambda b, n: (b, ch(n), 0)),
                pl.BlockSpec((8, LANE), lambda b, n: (0, 0))]
    args = [qkv, qkv, qkv, ab, prm]
    final = extra is not None
    if final:
        o_other, proj, z_col0, norm_w = extra
        zoff = z_col0 // wd
        in_specs += [pl.BlockSpec((None, c, wd), lambda b, n: (b, ch(n), 0)),
                     pl.BlockSpec((None, c, wd), lambda b, n: (b, ch(n), zoff)),
                     pl.BlockSpec((1, GDN_DK), lambda b, n: (0, 0))]
        args += [o_other, proj, norm_w.reshape(1, GDN_DK)]
    return pl.pallas_call(
        functools.partial(_gdn_kernel, rev=rev, final=final, nblk=nc),
        grid=(bsz, nc),
        in_specs=in_specs,
        out_specs=pl.BlockSpec((None, c, wd), lambda b, n: (b, ch(n), 0)),
        out_shape=jax.ShapeDtypeStruct((bsz, l_len, wd), BF16 if final else F32),
        scratch_shapes=[pltpu.VMEM((GDN_HEADS, GDN_DK, GDN_DK), F32)],
        compiler_params=_params("parallel", "arbitrary"),
        name="gdn_bwd" if rev else "gdn_fwd",
    )(*args)


def _diff_kernel(q_ref, k_ref, v_ref, lam_ref, nw_ref, o_ref, *, lam_init):
    key_ok = lax.broadcasted_iota(jnp.int32, (1, LEAD), 1) >= N_PAD
    v = v_ref[...]
    outs = []
    for m in range(2):
        sl = slice(m * DIFF_DQK, (m + 1) * DIFF_DQK)
        s = _dot_nt(q_ref[:, sl], k_ref[:, sl])
        s = jnp.concatenate([jnp.where(key_ok, s[:, :LEAD], NEG), s[:, LEAD:]], axis=1)
        e = jnp.exp2(s - jnp.max(s, axis=-1, keepdims=True))
        den = jnp.sum(e, axis=-1, keepdims=True)
        outs.append(_dot(e.astype(BF16), v) * (1.0 / den))
    lv = lam_ref[...]
    lam = (jnp.exp(jnp.sum(lv[0:1] * lv[1:2], axis=-1, keepdims=True))
           - jnp.exp(jnp.sum(lv[2:3] * lv[3:4], axis=-1, keepdims=True)) + lam_init)
    o = outs[0] - lam * outs[1]
    o = o * lax.rsqrt(jnp.mean(o * o, axis=-1, keepdims=True) + 1e-5) * nw_ref[...]
    o_ref[...] = (o * (1.0 - lam_init)).astype(o_ref.dtype)


def _diff_attention(qkv, lam_vecs, norm_w, lam_init, *, tq):
    bsz, l_len, _ = qkv.shape
    wq = 2 * DIFF_DQK
    nhd = DIFF_HEADS
    return pl.pallas_call(
        functools.partial(_diff_kernel, lam_init=lam_init),
        grid=(bsz, nhd, l_len // tq),
        in_specs=[pl.BlockSpec((None, tq, wq), lambda b, h, i: (b, i, h)),
                  pl.BlockSpec((None, l_len, wq), lambda b, h, i: (b, 0, nhd + h)),
                  pl.BlockSpec((None, l_len, DIFF_DV), lambda b, h, i: (b, 0, 2 * nhd + h)),
                  pl.BlockSpec((4, DIFF_DQK), lambda b, h, i: (0, 0)),
                  pl.BlockSpec((1, DIFF_DV), lambda b, h, i: (0, 0))],
        out_specs=pl.BlockSpec((None, tq, DIFF_DV), lambda b, h, i: (b, i, h)),
        out_shape=jax.ShapeDtypeStruct((bsz, l_len, DIFF_WIDTH), BF16),
        compiler_params=_params("parallel", "parallel", "parallel"),
        name="diff_attention",
    )(qkv, qkv, qkv, lam_vecs, norm_w.reshape(1, DIFF_DV))


def _swa_kernel(sink_ref, q_ref, k_ref, v_ref, o_ref):
    l_len = k_ref.shape[0]
    blk = SWA_BLOCK
    band = 3 * blk
    n = pl.program_id(1)
    start = pl.multiple_of(jnp.clip((n - 1) * blk, 0, l_len - band), blk)
    rows = SWA_GROUP * blk
    qpos = n * blk + (lax.broadcasted_iota(jnp.int32, (rows, band), 0) & (blk - 1))
    kpos = start + lax.broadcasted_iota(jnp.int32, (rows, band), 1)
    band_ok = (jnp.abs(qpos - kpos) <= SWA_WINDOW) & (kpos >= LEAD)
    meta_ok = lax.broadcasted_iota(jnp.int32, (rows, LEAD), 1) >= N_PAD
    kvs = range(SWA_KV_HEADS)
    hsl = [slice(h * HEAD_DIM, (h + 1) * HEAD_DIM) for h in kvs]
    q = [jnp.concatenate([q_ref[:, (h * SWA_GROUP + g) * HEAD_DIM:(h * SWA_GROUP + g + 1) * HEAD_DIM]
                          for g in range(SWA_GROUP)], axis=0) for h in kvs]
    sb = [jnp.where(band_ok, _dot_nt(q[h], k_ref[pl.ds(start, band), hsl[h]]), NEG) for h in kvs]
    sm = [jnp.where(meta_ok, _dot_nt(q[h], k_ref[0:LEAD, hsl[h]]), NEG) for h in kvs]
    sk = [jnp.concatenate([jnp.full((blk, 1), sink_ref[h * SWA_GROUP + g], F32) for g in range(SWA_GROUP)], axis=0)
          for h in kvs]
    mx = [jnp.maximum(jnp.maximum(jnp.max(sb[h], axis=-1, keepdims=True), jnp.max(sm[h], axis=-1, keepdims=True)),
                      sk[h]) for h in kvs]
    eb = [jnp.exp(sb[h] - mx[h]) for h in kvs]
    em = [jnp.exp(sm[h] - mx[h]) for h in kvs]
    den = [jnp.sum(eb[h], axis=-1, keepdims=True) + jnp.sum(em[h], axis=-1, keepdims=True) + jnp.exp(sk[h] - mx[h])
           for h in kvs]
    o = [(_dot(eb[h].astype(BF16), v_ref[pl.ds(start, band), hsl[h]])
          + _dot(em[h].astype(BF16), v_ref[0:LEAD, hsl[h]])) * (1.0 / den[h]) for h in kvs]
    for h in kvs:
        for g in range(SWA_GROUP):
            c0 = (h * SWA_GROUP + g) * HEAD_DIM
            o_ref[:, c0:c0 + HEAD_DIM] = o[h][g * blk:(g + 1) * blk].astype(o_ref.dtype)


def _window_attention(qkv, sink):
    bsz, l_len, _ = qkv.shape
    kv_w = SWA_KV_HEADS * HEAD_DIM
    koff = SWA_WIDTH // kv_w
    return pl.pallas_call(
        _swa_kernel,
        grid=(bsz, l_len // SWA_BLOCK),
        in_specs=[pl.BlockSpec(memory_space=pltpu.SMEM),
                  pl.BlockSpec((None, SWA_BLOCK, SWA_WIDTH), lambda b, i: (b, i, 0)),
                  pl.BlockSpec((None, l_len, kv_w), lambda b, i: (b, 0, koff)),
                  pl.BlockSpec((None, l_len, kv_w), lambda b, i: (b, 0, koff + 1))],
        out_specs=pl.BlockSpec((None, SWA_BLOCK, SWA_WIDTH), lambda b, i: (b, i, 0)),
        out_shape=jax.ShapeDtypeStruct((bsz, l_len, SWA_WIDTH), BF16),
        compiler_params=_params("parallel", "parallel"),
        name="window_attention",
    )(sink, qkv, qkv, qkv)


def _ssd_kernel(*refs, rev, final, nc):
    if final:
        x_ref, b_ref, c_ref, dt_ref, prm_ref, yb_ref, z0_ref, z1_ref, drow_ref, nw_ref, o_ref, s_ref = refs
    else:
        x_ref, b_ref, c_ref, dt_ref, prm_ref, o_ref, s_ref = refs
    q = SSD_CHUNK
    n = pl.program_id(1)

    @pl.when(n == 0)
    def _():
        s_ref[...] = jnp.zeros(s_ref.shape, F32)

    chunk = (nc - 1 - n) if rev else n
    rows = chunk * q + lax.broadcasted_iota(jnp.int32, (q, 1), 0)
    valid = rows >= N_PAD
    d = 1 if rev else 0
    p = SSD_HEADDIM
    gh = SSD_GROUP_HEADS
    gw = SSD_GROUP_WIDTH

    dt = jnp.where(valid, _softplus(dt_ref[...] + prm_ref[1:2, :]), 0.0)
    da = dt * (-jnp.exp(prm_ref[0:1, :]))
    ri = lax.broadcasted_iota(jnp.int32, (q, q), 0)
    ci = lax.broadcasted_iota(jnp.int32, (q, q), 1)
    incl = (ri <= ci) if rev else (ri >= ci)
    eye = (ri == ci).astype(F32)
    cum = _dot(incl.astype(F32), da, HIGHEST)
    cum_t = _dot_tn(cum, eye, HIGHEST)
    tot = cum[0:1, :] if rev else cum[q - 1:q, :]
    ecum = jnp.exp(cum)
    dstate = jnp.exp(tot - cum)
    etot = jnp.exp(tot)

    lane_lo = lax.broadcasted_iota(jnp.int32, (1, LANE), 1) < p
    groups = range(SSD_GROUPS)
    pairs = range(SSD_HEADS // 2)
    ppg = gh // 2
    col0 = [d * SSD_HEADS + 2 * j for j in pairs]

    def pair_cols(arr, j):
        return jnp.where(lane_lo, arr[:, col0[j]:col0[j] + 1], arr[:, col0[j] + 1:col0[j] + 2])

    def seg(col):
        return jnp.where(incl, jnp.exp(jnp.where(incl, cum[:, col:col + 1] - cum_t[col:col + 1, :], 0.0)), 0.0)

    bg = [b_ref[:, g * SSD_STATE:(g + 1) * SSD_STATE].astype(BF16) for g in groups]
    cg = [c_ref[:, g * SSD_STATE:(g + 1) * SSD_STATE].astype(BF16) for g in groups]
    sg = [s_ref[g * gw:(g + 1) * gw, :] for g in groups]
    cb = [_dot_nt(cg[g], bg[g]) for g in groups]
    y_off = [_dot_nt(cg[g], sg[g].astype(BF16)) for g in groups]
    xdt = [x_ref[:, j * LANE:(j + 1) * LANE] * pair_cols(dt, j) for j in pairs]
    lhs = [jnp.concatenate([cb[j // ppg] * seg(col0[j]), cb[j // ppg] * seg(col0[j] + 1)], axis=1) for j in pairs]
    rhs = [jnp.concatenate([jnp.where(lane_lo, xdt[j], 0.0), jnp.where(lane_lo, 0.0, xdt[j])], axis=0)
           for j in pairs]
    y_diag = [_bdot(lhs[j], rhs[j]) for j in pairs]
    ys = [y_diag[j] + y_off[j // ppg][:, (j % ppg) * LANE:(j % ppg + 1) * LANE] * pair_cols(ecum, j) for j in pairs]
    xs = [xdt[j] * pair_cols(dstate, j) for j in pairs]
    st = [_bdot_tn(jnp.concatenate(xs[g * ppg:(g + 1) * ppg], axis=1), bg[g]) for g in groups]
    for g in groups:
        decs = [jnp.broadcast_to(etot[:, c_:c_ + 1], (p, 1))
                for c_ in range(d * SSD_HEADS + g * gh, d * SSD_HEADS + (g + 1) * gh)]
        s_ref[g * gw:(g + 1) * gw, :] = sg[g] * jnp.concatenate(decs, axis=0) + st[g]
    y = jnp.concatenate(ys, axis=1)
    if final:
        y = y + yb_ref[...] + x_ref[...] * drow_ref[...]
        y = y * _silu(jnp.concatenate([z0_ref[...], z1_ref[...]], axis=1))
        for gi in range(SSD_GROUPS):
            sl = slice(gi * gw, (gi + 1) * gw)
            yg = y[:, sl]
            yg = yg * lax.rsqrt(jnp.mean(yg * yg, axis=-1, keepdims=True) + EPS) * nw_ref[:, sl]
            o_ref[:, sl] = yg.astype(o_ref.dtype)
    else:
        o_ref[...] = y


def _ssd_pass(xbc, dt_raw, prm, *, rev, extra=None):
    bsz, l_len, _ = xbc.shape
    q = SSD_CHUNK
    nc = l_len // q
    wd = SSD_WIDTH
    gs = SSD_GROUPS * SSD_STATE
    ch = (lambda n: nc - 1 - n) if rev else (lambda n: n)
    in_specs = [pl.BlockSpec((None, q, wd), lambda b, n: (b, ch(n), 0)),
                pl.BlockSpec((None, q, gs), lambda b, n: (b, ch(n), wd // gs)),
                pl.BlockSpec((None, q, gs), lambda b, n: (b, ch(n), wd // gs + 1)),
                pl.BlockSpec((None, q, LANE), lambda b, n: (b, ch(n), 0)),
                pl.BlockSpec((8, LANE), lambda b, n: (0, 0))]
    args = [xbc, xbc, xbc, dt_raw, prm]
    final = extra is not None
    if final:
        y_other, proj, z_col0, d_row, norm_w = extra
        zw = wd // 2
        zoff = z_col0 // zw
        in_specs += [pl.BlockSpec((None, q, wd), lambda b, n: (b, ch(n), 0)),
                     pl.BlockSpec((None, q, zw), lambda b, n: (b, ch(n), zoff)),
                     pl.BlockSpec((None, q, zw), lambda b, n: (b, ch(n), zoff + 1)),
                     pl.BlockSpec((1, wd), lambda b, n: (0, 0)),
                     pl.BlockSpec((1, wd), lambda b, n: (0, 0))]
        args += [y_other, proj, proj, d_row, norm_w.reshape(1, wd)]
    return pl.pallas_call(
        functools.partial(_ssd_kernel, rev=rev, final=final, nc=nc),
        grid=(bsz, nc),
        in_specs=in_specs,
        out_specs=pl.BlockSpec((None, q, wd), lambda b, n: (b, ch(n), 0)),
        out_shape=jax.ShapeDtypeStruct((bsz, l_len, wd), BF16 if final else F32),
        scratch_shapes=[pltpu.VMEM((wd, SSD_STATE), F32)],
        compiler_params=_params("parallel", "arbitrary"),
        name="ssd_bwd" if rev else "ssd_fwd",
    )(*args)


def _lane_params(*rows):
    out = jnp.zeros((8, LANE), F32)
    for i, r in enumerate(rows):
        r = r.astype(F32).reshape(-1)
        out = out.at[i, :r.shape[0]].set(r)
    return out


def _pad_cols(w, n):
    return jnp.pad(w, ((0, 0), (0, n - w.shape[1])))


def _diff_lambda_init(layer):
    return 0.8 - 0.6 * math.exp(-0.3 * layer)


def _tiles(l_pad):
    return _tile(l_pad, 1056, 16)


def _tall_tile(l_pad):
    return _tile(l_pad, 2112, 16)


def _even_mixer(h, hw, ssq, w_in, j, conv_w, a_log, dt_bias, gdn_norm_w, lam_vecs, diff_norm_w, w_out, lam_init,
                tables, next_norm_w):
    bsz, l_pad, d = h.shape
    tm = _tiles(l_pad)
    qkv_w = 3 * GDN_WIDTH
    main_w = qkv_w + GDN_WIDTH
    ab_w = 4 * GDN_HEADS
    wt = jnp.swapaxes(w_in, 1, 2)
    w_tail, w_ab = _tail_cast(wt, j, main_w, ab_w, wt.shape[1] - main_w - ab_w)
    tall = _tall_tile(l_pad)
    proj_a = _matmul([(hw, wt, j, 0, 0, True)], n=main_w, out_dtype=F32, tm=tall, tn=_tile(main_w, 512),
                     a_single=True, ssq=ssq, name="even_in_proj_a")
    proj_b = _matmul([(hw, w_tail, 0, 0, 0, True)], out_dtype=F32, tm=tall, tn=_tile(w_tail.shape[1], 512),
                     a_single=True, ssq=ssq, name="even_in_proj_b")
    ab = _matmul([(hw, w_ab, 0, 0, 0, True)], out_dtype=F32, tm=tm, tn=LANE, ssq=ssq, name="even_ab_proj")

    qkv_a = _conv_silu(proj_a, 0, qkv_w, conv_w, None)
    prm = _lane_params(a_log, dt_bias)
    o_bwd = _gdn_pass(qkv_a, ab, prm, rev=True)
    o_a = _gdn_pass(qkv_a, ab, prm, rev=False, extra=(o_bwd, proj_a, qkv_w, gdn_norm_w))

    qk_w = 2 * DIFF_HEADS * DIFF_DQK
    cols = ((0, qk_w), (qk_w, qk_w), (2 * qk_w, DIFF_WIDTH))
    qkv_b = _rope_cast(proj_b, cols, tables, tr=_tile(l_pad, 384, 16), q_scale=DIFF_DQK ** -0.5 * LOG2_E)
    o_b = _diff_attention(qkv_b, lam_vecs, diff_norm_w, lam_init, tq=_tile(l_pad, 384, 16))

    return _matmul([(o_a, w_out, j, 0, 0, False), (o_b, w_out, j, GDN_WIDTH, 0, False)], out_dtype=F32, tm=tm,
                   tn=_tile(d, 512), res=h, zero_pad=True, norm_w=next_norm_w, name="even_out_proj")


def _odd_mixer(h, hw, ssq, w_in, j, sink, conv_w, conv_b, a_log, dt_bias, d_skip, ssd_norm_w, w_out, tables,
               next_norm_w):
    bsz, l_pad, d = h.shape
    tm = _tiles(l_pad)
    kv_w = SWA_KV_HEADS * HEAD_DIM
    z_col = SWA_WIDTH + 2 * kv_w
    xbc_col = z_col + SSD_WIDTH
    main_w = xbc_col + SSD_XBC
    wt = jnp.swapaxes(w_in, 1, 2)
    w_dt = jnp.pad(wt[j, main_w:], ((0, LANE - (wt.shape[1] - main_w)), (0, 0))).astype(BF16)[None]
    proj = _matmul([(hw, wt, j, 0, 0, True)], n=main_w, out_dtype=F32, tm=_tall_tile(l_pad),
                   tn=_tile(main_w, 512), a_single=True, ssq=ssq, name="odd_in_proj")
    dt_raw = _matmul([(hw, w_dt, 0, 0, 0, True)], out_dtype=F32, tm=tm, tn=LANE, ssq=ssq, name="odd_dt_proj")

    cols = ((0, SWA_WIDTH), (SWA_WIDTH, kv_w), (SWA_WIDTH + kv_w, kv_w))
    qkv_c = _rope_cast(proj, cols, tables, tr=_tile(l_pad, 384, 16), q_scale=HEAD_DIM ** -0.5)
    o_c = _window_attention(qkv_c, sink.astype(F32))

    xbc = _conv_silu(proj, xbc_col, SSD_XBC, conv_w, conv_b)
    prm = _lane_params(a_log, dt_bias)
    d_row = jnp.repeat(d_skip.astype(F32), SSD_HEADDIM).reshape(1, SSD_WIDTH)
    y_bwd = _ssd_pass(xbc, dt_raw, prm, rev=True)
    y = _ssd_pass(xbc, dt_raw, prm, rev=False, extra=(y_bwd, proj, z_col, d_row, ssd_norm_w))

    return _matmul([(o_c, w_out, j, 0, 0, False), (y, w_out, j, SWA_WIDTH, 0, False)], out_dtype=F32, tm=tm,
                   tn=_tile(d, 512), res=h, zero_pad=True, norm_w=next_norm_w, name="odd_out_proj")


def _ffn(h, hw, ssq, w_gate, w_up, w_down_bf16, layer, next_norm_w):
    bsz, l_pad, d = h.shape
    f = w_gate.shape[2]
    act = _swiglu_up(hw, ssq, w_gate, w_up, layer, tm=_tall_tile(l_pad), tn=_tile(f, 256))
    return _matmul([(act, w_down_bf16, layer, 0, 0, False)], out_dtype=F32, tm=_tile(l_pad, 704, 16),
                   tn=_tile(d, 256), res=h, norm_w=next_norm_w, name="ffn_down")


def kernel(x, meta_tokens, norm_mix, norm_ffn, norm_final, even_w_in, even_conv, gdn_a_log, gdn_dt_bias, gdn_norm, diff_lam_q1, diff_lam_k1, diff_lam_q2, diff_lam_k2, diff_norm, even_w_out, odd_w_in, swa_sink, ssd_conv_w, ssd_conv_b, ssd_a_log, ssd_dt_bias, ssd_d, ssd_norm, odd_w_out, ffn_w_gate, ffn_w_up, ffn_w_down):
    bsz, seq, d = x.shape
    depth = norm_mix.shape[0]
    l_pad = LEAD + seq
    tables = _rope_tables(l_pad)
    w_down_bf16 = ffn_w_down.astype(BF16)
    h, hw, ssq = _embed_norm(x, meta_tokens, norm_mix[0])
    for i in range(depth):
        j = i // 2
        if i % 2 == 0:
            lam_vecs = jnp.stack([diff_lam_q1[j], diff_lam_k1[j], diff_lam_q2[j], diff_lam_k2[j]]).astype(F32)
            h, hw, ssq = _even_mixer(h, hw, ssq, even_w_in, j, even_conv[j], gdn_a_log[j], gdn_dt_bias[j],
                                     gdn_norm[j], lam_vecs, diff_norm[j], even_w_out, _diff_lambda_init(i), tables,
                                     norm_ffn[i])
        else:
            h, hw, ssq = _odd_mixer(h, hw, ssq, odd_w_in, j, swa_sink[j], ssd_conv_w[j], ssd_conv_b[j], ssd_a_log[j],
                                    ssd_dt_bias[j], ssd_d[j], ssd_norm[j], odd_w_out, tables, norm_ffn[i])
        if i + 1 < depth:
            h, hw, ssq = _ffn(h, hw, ssq, ffn_w_gate, ffn_w_up, w_down_bf16, i, norm_mix[i + 1])
        else:
            h = _ffn(h, hw, ssq, ffn_w_gate, ffn_w_up, w_down_bf16, i, None)
    return _rmsnorm(h, norm_final, out_dtype=x.dtype, row0=LEAD, rows=seq, tr=LANE)
```

```python
import functools
import math

import jax
import jax.numpy as jnp
from jax import lax
from jax.experimental import pallas as pl
from jax.experimental.pallas import tpu as pltpu

F32 = jnp.float32
BF16 = jnp.bfloat16
HIGHEST = lax.Precision.HIGHEST

N_META = 16
LEAD = 128
N_PAD = LEAD - N_META
HEAD_DIM = 128
ROT_DIM = HEAD_DIM // 4
ROPE_THETA = 500000.0
CONV_W = 7
EPS = 1e-6

GDN_HEADS = 16
GDN_DK = 128
GDN_WIDTH = 2048
GDN_CHUNK = 64
GDN_STEP_CHUNKS = 2
GDN_GROUP_HEADS = 4
DIFF_HEADS = 8
DIFF_DQK = 128
DIFF_DV = 256
DIFF_WIDTH = 2048
SWA_HEADS = 16
SWA_KV_HEADS = 4
SWA_GROUP = 4
SWA_WIDTH = 2048
SWA_WINDOW = 128
SWA_BLOCK = 128
SSD_WIDTH = 2048
SSD_HEADDIM = 64
SSD_HEADS = 32
SSD_STATE = 128
SSD_GROUPS = 4
SSD_GROUP_HEADS = SSD_HEADS // SSD_GROUPS
SSD_GROUP_WIDTH = SSD_WIDTH // SSD_GROUPS
SSD_XBC = 3072
SSD_CHUNK = 128

LANE = 128
NEG = -1e30
LOG2_E = math.log2(math.e)
VMEM_LIMIT = 56 * 1024 * 1024


def _params(*sem):
    return pltpu.CompilerParams(dimension_semantics=sem, vmem_limit_bytes=VMEM_LIMIT)


def _tile(n, pref, align=LANE):
    if n <= pref:
        return n
    t = (pref // align) * align
    while t >= align:
        if n % t == 0:
            return t
        t -= align
    return n


def _dot(a, b, precision=None):
    return jnp.dot(a, b, preferred_element_type=F32, precision=precision)


def _dot_nt(a, b, precision=None):
    return lax.dot_general(a, b, (((1,), (1,)), ((), ())), preferred_element_type=F32, precision=precision)


def _dot_tn(a, b, precision=None):
    return lax.dot_general(a, b, (((0,), (0,)), ((), ())), preferred_element_type=F32, precision=precision)


def _sigmoid(x):
    return 1.0 / (1.0 + jnp.exp(-x))


def _silu(x):
    return x * _sigmoid(x)


def _softplus(x):
    return jnp.maximum(x, 0.0) + jnp.log1p(jnp.exp(-jnp.abs(x)))


def _rmsnorm_kernel(x_ref, w_ref, o_ref, *, eps):
    x = x_ref[...]
    ms = jnp.mean(x * x, axis=-1, keepdims=True)
    o_ref[...] = (x * lax.rsqrt(ms + eps) * w_ref[...]).astype(o_ref.dtype)


def _rmsnorm(h, w, *, out_dtype, row0=0, rows=None, tr):
    bsz, l_len, d = h.shape
    rows = l_len - row0 if rows is None else rows
    off = row0 // tr
    return pl.pallas_call(
        functools.partial(_rmsnorm_kernel, eps=EPS),
        grid=(bsz, rows // tr),
        in_specs=[pl.BlockSpec((None, tr, d), lambda b, i: (b, i + off, 0)),
                  pl.BlockSpec((1, d), lambda b, i: (0, 0))],
        out_specs=pl.BlockSpec((None, tr, d), lambda b, i: (b, i, 0)),
        out_shape=jax.ShapeDtypeStruct((bsz, rows, d), out_dtype),
        compiler_params=_params("parallel", "parallel"),
        name="rmsnorm",
    )(h, w.reshape(1, d))


def _embed_kernel(x_ref, meta_ref, w_ref, h_ref, hw_ref, ssq_ref):
    def emit(blk):
        h_ref[...] = blk
        hw_ref[...] = (blk * w_ref[...]).astype(hw_ref.dtype)
        ssq_ref[...] = jnp.broadcast_to(jnp.sum(blk * blk, axis=-1, keepdims=True), ssq_ref.shape)

    i = pl.program_id(1)

    @pl.when(i == 0)
    def _():
        emit(jnp.concatenate([jnp.zeros((N_PAD, meta_ref.shape[-1]), F32), meta_ref[...]], axis=0))

    @pl.when(i > 0)
    def _():
        emit(x_ref[...])


def _embed_norm(x, meta, w):
    bsz, seq, d = x.shape
    l_pad = LEAD + seq
    blk = pl.BlockSpec((None, LEAD, d), lambda b, i: (b, i, 0))
    return pl.pallas_call(
        _embed_kernel,
        grid=(bsz, l_pad // LEAD),
        in_specs=[pl.BlockSpec((None, LEAD, d), lambda b, i: (b, jnp.maximum(i - 1, 0), 0)),
                  pl.BlockSpec((N_META, d), lambda b, i: (0, 0)),
                  pl.BlockSpec((1, d), lambda b, i: (0, 0))],
        out_specs=[blk, blk, pl.BlockSpec((None, LEAD, LANE), lambda b, i: (b, i, 0))],
        out_shape=[jax.ShapeDtypeStruct((bsz, l_pad, d), F32), jax.ShapeDtypeStruct((bsz, l_pad, d), BF16),
                   jax.ShapeDtypeStruct((bsz, l_pad, LANE), F32)],
        compiler_params=_params("parallel", "parallel"),
        name="embed_norm",
    )(x, meta.astype(F32), w.reshape(1, d))


def _tail_cast_kernel(x_ref, tail_ref, ab_ref, *, ab_w):
    tail_w, tk = tail_ref.shape
    tail_ref[...] = x_ref[ab_w:ab_w + tail_w, :].astype(tail_ref.dtype)
    ab = jnp.concatenate([x_ref[:ab_w, :], jnp.zeros((LANE - ab_w, tk), F32)], axis=0)
    ab_ref[...] = ab.astype(ab_ref.dtype)


def _tail_cast(wt, layer, row0, ab_w, tail_w):
    _, n_total, kdim = wt.shape
    win = row0
    assert row0 + ab_w + tail_w == n_total and ab_w + tail_w <= win
    tk = _tile(kdim, 256)
    return pl.pallas_call(
        functools.partial(_tail_cast_kernel, ab_w=ab_w),
        grid=(kdim // tk,),
        in_specs=[pl.BlockSpec((None, win, tk), lambda k: (layer, 1, k))],
        out_specs=[pl.BlockSpec((None, tail_w, tk), lambda k: (0, 0, k)),
                   pl.BlockSpec((None, LANE, tk), lambda k: (0, 0, k))],
        out_shape=[jax.ShapeDtypeStruct((1, tail_w, kdim), BF16), jax.ShapeDtypeStruct((1, LANE, kdim), BF16)],
        compiler_params=_params("parallel"),
        name="tail_cast",
    )(wt)


def _inv_rms(ssq_ref, d_model):
    return lax.rsqrt(ssq_ref[:, 0:1] * (1.0 / d_model) + EPS)


def _mm_kernel(*refs, trans, has_res, has_scale, zero_pad, emit_norm, rope, tm, d_model):
    n_pairs = len(trans)
    ins = refs[:2 * n_pairs]
    pos = 2 * n_pairs
    res_ref = refs[pos] if has_res else None
    pos += int(has_res)
    ssq_in_ref = refs[pos] if has_scale else None
    pos += int(has_scale)
    nw_ref = refs[pos] if emit_norm else None
    pos += int(emit_norm)
    rope_refs = refs[pos:pos + 3] if rope else None
    pos += 3 if rope else 0
    o_ref = refs[pos]

    acc = None
    for p in range(n_pairs):
        w = ins[2 * p + 1][...].astype(BF16)
        d = _dot_nt(ins[2 * p][...], w) if trans[p] else _dot(ins[2 * p][...], w)
        acc = d if acc is None else acc + d
    if has_scale:
        acc = acc * _inv_rms(ssq_in_ref, d_model)
    if has_res:
        acc = acc + res_ref[...]
    if zero_pad:
        row = pl.program_id(1) * tm + lax.broadcasted_iota(jnp.int32, acc.shape, 0)
        acc = jnp.where(row < N_PAD, 0.0, acc)
    if rope:
        n_q, n_rot, q_scale = rope
        jt = pl.program_id(2)

        def rotated(scale):
            cf, s1, s2 = (r[...] for r in rope_refs)
            for hd in range(acc.shape[-1] // HEAD_DIM):
                sl = slice(hd * HEAD_DIM, (hd + 1) * HEAD_DIM)
                o_ref[:, sl] = (_rope(acc[:, sl], cf, s1, s2) * scale).astype(o_ref.dtype)

        @pl.when(jt < n_q)
        def _():
            rotated(q_scale)

        @pl.when((jt >= n_q) & (jt < n_rot))
        def _():
            rotated(1.0)

        @pl.when(jt >= n_rot)
        def _():
            o_ref[...] = acc.astype(o_ref.dtype)
    else:
        o_ref[...] = acc.astype(o_ref.dtype)
    if emit_norm:
        hw_ref, ssq_ref = refs[pos + 1], refs[pos + 2]
        hw_ref[...] = (acc * nw_ref[...]).astype(hw_ref.dtype)
        part = jnp.broadcast_to(jnp.sum(acc * acc, axis=-1, keepdims=True), ssq_ref.shape)
        j = pl.program_id(2)

        @pl.when(j == 0)
        def _():
            ssq_ref[...] = part

        @pl.when(j > 0)
        def _():
            ssq_ref[...] += part


def _row_spec(block, index_map, single_buffer):
    if single_buffer:
        return pl.BlockSpec(block, index_map, pipeline_mode=pl.Buffered(1))
    return pl.BlockSpec(block, index_map)


def _matmul(pairs, *, out_dtype, tm, tn, n=None, res=None, zero_pad=False, a_single=False, ssq=None, norm_w=None,
            rope=None, name):
    bsz, l_len, kdim = pairs[0][0].shape
    if n is None:
        n = pairs[0][1].shape[1 if pairs[0][5] else 2]
    in_specs, args = [], []
    for a, w, layer, row0, col0, trans in pairs:
        assert row0 % kdim == 0 and col0 % tn == 0
        r, c = row0 // kdim, col0 // tn
        w_spec = (pl.BlockSpec((None, tn, kdim), lambda b_, i, j, ly=layer, r=r, c=c: (ly, j + c, r)) if trans else
                  pl.BlockSpec((None, kdim, tn), lambda b_, i, j, ly=layer, r=r, c=c: (ly, r, j + c)))
        in_specs += [_row_spec((None, tm, kdim), lambda b_, i, j: (b_, i, 0), a_single), w_spec]
        args += [a, w]
    tile = pl.BlockSpec((None, tm, tn), lambda b_, i, j: (b_, i, j))
    stat = pl.BlockSpec((None, tm, LANE), lambda b_, i, j: (b_, i, 0))
    if res is not None:
        in_specs.append(tile)
        args.append(res)
    if ssq is not None:
        in_specs.append(stat)
        args.append(ssq)
    out_specs, out_shape = tile, jax.ShapeDtypeStruct((bsz, l_len, n), out_dtype)
    if norm_w is not None:
        in_specs.append(pl.BlockSpec((1, tn), lambda b_, i, j: (0, j)))
        args.append(norm_w.reshape(1, n))
        out_specs = [tile, tile, stat]
        out_shape = [out_shape, jax.ShapeDtypeStruct((bsz, l_len, n), BF16),
                     jax.ShapeDtypeStruct((bsz, l_len, LANE), F32)]
    rope_cfg = None
    if rope is not None:
        tables, q_w, k_w, q_scale = rope
        assert q_w % tn == 0 and k_w % tn == 0 and tn % HEAD_DIM == 0
        in_specs += [pl.BlockSpec((tm, LANE), lambda b_, i, j: (i, 0))] * 3
        args += list(tables)
        rope_cfg = (q_w // tn, (q_w + k_w) // tn, q_scale)
    return pl.pallas_call(
        functools.partial(_mm_kernel, trans=tuple(p[5] for p in pairs), has_res=res is not None,
                          has_scale=ssq is not None, zero_pad=zero_pad, emit_norm=norm_w is not None,
                          rope=rope_cfg, tm=tm, d_model=kdim),
        grid=(bsz, l_len // tm, n // tn),
        in_specs=in_specs,
        out_specs=out_specs,
        out_shape=out_shape,
        compiler_params=_params("parallel", "parallel", "arbitrary"),
        name=name,
    )(*args)


def _swiglu_kernel(a_ref, g_ref, u_ref, ssq_ref, o_ref):
    a = a_ref[...]
    r = _inv_rms(ssq_ref, a.shape[-1])
    g = _dot(a, g_ref[...].astype(BF16)) * r
    u = _dot(a, u_ref[...].astype(BF16)) * r
    o_ref[...] = (_silu(g) * u).astype(o_ref.dtype)


def _swiglu_up(a, ssq, w_gate, w_up, layer, *, tm, tn):
    bsz, l_len, kdim = a.shape
    n = w_gate.shape[2]
    return pl.pallas_call(
        _swiglu_kernel,
        grid=(bsz, l_len // tm, n // tn),
        in_specs=[_row_spec((None, tm, kdim), lambda b, i, j: (b, i, 0), True),
                  pl.BlockSpec((None, kdim, tn), lambda b, i, j: (layer, 0, j)),
                  pl.BlockSpec((None, kdim, tn), lambda b, i, j: (layer, 0, j)),
                  pl.BlockSpec((None, tm, LANE), lambda b, i, j: (b, i, 0))],
        out_specs=pl.BlockSpec((None, tm, tn), lambda b, i, j: (b, i, j)),
        out_shape=jax.ShapeDtypeStruct((bsz, l_len, n), BF16),
        compiler_params=_params("parallel", "parallel", "parallel"),
        name="swiglu_up",
    )(a, w_gate, w_up, ssq)


def _conv_kernel(*refs, l_len, rc, has_bias):
    if has_bias:
        x_ref, w_ref, b_ref, o_ref, xp_ref = refs
    else:
        x_ref, w_ref, o_ref, xp_ref = refs
        b_ref = None
    tc = x_ref.shape[-1]
    halo = 8
    xp_ref[0:halo, :] = jnp.zeros((halo, tc), F32)
    xp_ref[l_len + halo:l_len + 2 * halo, :] = jnp.zeros((halo, tc), F32)
    xp_ref[halo:l_len + halo, :] = x_ref[...]
    w = w_ref[...]
    reach = CONV_W // 2
    for c in range(l_len // rc):
        r0 = c * rc
        acc = None
        for j in range(CONV_W):
            s = r0 + halo - reach + j
            t = xp_ref[s:s + rc, :] * w[j:j + 1, :]
            acc = t if acc is None else acc + t
        if has_bias:
            acc = acc + b_ref[...]
        y = _silu(acc)
        if r0 < N_PAD:
            row = r0 + lax.broadcasted_iota(jnp.int32, y.shape, 0)
            y = jnp.where(row < N_PAD, 0.0, y)
        o_ref[r0:r0 + rc, :] = y


def _conv_silu(proj, col0, width, w, bias, *, tc=256, rc=128):
    bsz, l_len, _ = proj.shape
    off = col0 // tc
    in_specs = [pl.BlockSpec((None, l_len, tc), lambda b, j: (b, 0, j + off)),
                pl.BlockSpec((CONV_W, tc), lambda b, j: (0, j))]
    args = [proj, w]
    if bias is not None:
        in_specs.append(pl.BlockSpec((1, tc), lambda b, j: (0, j)))
        args.append(bias.reshape(1, width))
    return pl.pallas_call(
        functools.partial(_conv_kernel, l_len=l_len, rc=rc, has_bias=bias is not None),
        grid=(bsz, width // tc),
        in_specs=in_specs,
        out_specs=pl.BlockSpec((None, l_len, tc), lambda b, j: (b, 0, j)),
        out_shape=jax.ShapeDtypeStruct((bsz, l_len, width), F32),
        scratch_shapes=[pltpu.VMEM((l_len + 16, tc), F32)],
        compiler_params=_params("parallel", "parallel"),
        name="conv_silu",
    )(*args)


def _rope(x, cf, s1, s2):
    half = ROT_DIM // 2
    return x * cf + pltpu.roll(x, LANE - half, 1) * s1 + pltpu.roll(x, half, 1) * s2


def _rope_tables(l_pad):
    half = ROT_DIM // 2
    inv = ROPE_THETA ** (-jnp.arange(0, ROT_DIM, 2, dtype=F32) / ROT_DIM)
    pos = (jnp.arange(l_pad) - N_PAD).astype(F32)
    ang = pos[:, None] * inv[None, :]
    cos, sin = jnp.cos(ang), jnp.sin(ang)
    ones = jnp.ones((l_pad, LANE - ROT_DIM), F32)
    zeros = jnp.zeros((l_pad, LANE - ROT_DIM), F32)
    zh = jnp.zeros((l_pad, half), F32)
    cf = jnp.concatenate([cos, cos, ones], axis=1)
    s1 = jnp.concatenate([-sin, zh, zeros], axis=1)
    s2 = jnp.concatenate([zh, sin, zeros], axis=1)
    return cf, s1, s2


def _bdot(a, b):
    return _dot(a.astype(BF16), b.astype(BF16))


def _bdot_nt(a, b):
    return _dot_nt(a.astype(BF16), b.astype(BF16))


def _bdot_tn(a, b):
    return _dot_tn(a.astype(BF16), b.astype(BF16))


def _gdn_kernel(*refs, rev, final, nblk):
    if final:
        q_ref, k_ref, v_ref, ab_ref, prm_ref, ob_ref, z_ref, nw_ref, o_ref, s_ref = refs
    else:
        q_ref, k_ref, v_ref, ab_ref, prm_ref, o_ref, s_ref = refs
    c = GDN_CHUNK
    nsub = q_ref.shape[0] // c
    n = pl.program_id(1)

    @pl.when(n == 0)
    def _():
        s_ref[...] = jnp.zeros(s_ref.shape, F32)

    blk = (nblk - 1 - n) if rev else n
    d = 1 if rev else 0
    nh = GDN_HEADS
    heads = range(nh)
    sls = [slice(h * GDN_DK, (h + 1) * GDN_DK) for h in heads]
    cols = [d * nh + h for h in heads]

    ri = lax.broadcasted_iota(jnp.int32, (c, c), 0)
    ci = lax.broadcasted_iota(jnp.int32, (c, c), 1)
    incl = (ri <= ci) if rev else (ri >= ci)
    strict = (ri < ci) if rev else (ri > ci)
    eye = (ri == ci).astype(F32)

    rsl, cums, cum_ts, tots, betas = [], [], [], [], []
    for sub in range(nsub):
        rs = slice(sub * c, (sub + 1) * c)
        rows = (blk * nsub + sub) * c + lax.broadcasted_iota(jnp.int32, (c, 1), 0)
        valid = rows >= N_PAD
        ab = ab_ref[rs, :]
        g_all = -jnp.exp(prm_ref[0:1, :]) * _softplus(ab + prm_ref[1:2, :])
        g_all = jnp.where(valid, g_all, 0.0)
        cum = _dot(incl.astype(F32), g_all, HIGHEST)
        rsl.append(rs)
        cums.append(cum)
        cum_ts.append(_dot_tn(cum, eye, HIGHEST))
        tots.append(cum[0:1, :] if rev else cum[c - 1:c, :])
        betas.append(jnp.where(valid, _sigmoid(ab), 0.0))
    n_sq = int(math.log2(c)) - 1
    scan_order = range(nsub - 1, -1, -1) if rev else range(nsub)

    def head_group(hs):
        units = [(sub, h) for sub in range(nsub) for h in hs]
        ids = range(len(units))
        gcol = [cums[sub][:, cols[h]:cols[h] + 1] for sub, h in units]
        tot_h = [tots[sub][:, cols[h]:cols[h] + 1] for sub, h in units]
        bcol = [betas[sub][:, 2 * nh + cols[h]:2 * nh + cols[h] + 1] for sub, h in units]
        egc = [jnp.exp(g_) for g_ in gcol]
        qn, kn = [], []
        for sub, h in units:
            qh, kh = q_ref[rsl[sub], sls[h]], k_ref[rsl[sub], sls[h]]
            qn.append(qh * lax.rsqrt(jnp.sum(qh * qh, axis=-1, keepdims=True) + 1e-6) * (GDN_DK ** -0.5))
            kn.append(kh * lax.rsqrt(jnp.sum(kh * kh, axis=-1, keepdims=True) + 1e-6))
        kb = [kn[u] * bcol[u] for u in ids]
        vb = [v_ref[rsl[sub], sls[h]] * bcol[u] for u, (sub, h) in enumerate(units)]
        decay = [jnp.where(incl, jnp.exp(jnp.where(incl, gcol[u] - cum_ts[sub][cols[h]:cols[h] + 1, :], 0.0)), 0.0)
                 for u, (sub, h) in enumerate(units)]
        yield
        kq = [_bdot_nt(jnp.concatenate([kb[u], qn[u]], axis=0), kn[u]) for u in ids]
        yield
        lower = [jnp.where(strict, kq[u][:c] * decay[u], 0.0) for u in ids]
        qk = [kq[u][c:] * decay[u] for u in ids]
        x = [eye - lower[u] for u in ids]
        p = [_bdot(lower[u], lower[u]) for u in ids]
        yield
        for t in range(n_sq):
            x = [x[u] + _bdot(x[u], p[u]) for u in ids]
            if t + 1 < n_sq:
                p = [_bdot(p[u], p[u]) for u in ids]
            yield
        uw = [_bdot(x[u], jnp.concatenate([vb[u], kb[u] * egc[u]], axis=1)) for u in ids]
        yield
        s_cur = [s_ref[h] for h in hs]
        for sub in scan_order:
            us = [sub * len(hs) + i for i in range(len(hs))]
            wq = [_bdot(jnp.concatenate([uw[u][:, GDN_DK:], qn[u] * egc[u]], axis=0), s_cur[i])
                  for i, u in enumerate(us)]
            yield
            v_new = [uw[u][:, :GDN_DK] - wq[i][:c] for i, u in enumerate(us)]
            o = [wq[i][c:] + _bdot(qk[u], v_new[i]) for i, u in enumerate(us)]
            kv = [_bdot_tn(kn[u] * jnp.exp(tot_h[u] - gcol[u]), v_new[i]) for i, u in enumerate(us)]
            yield
            s_cur = [s_cur[i] * jnp.exp(tot_h[u]) + kv[i] for i, u in enumerate(us)]
            for i, h in enumerate(hs):
                oh = o[i]
                if final:
                    oh = oh + ob_ref[rsl[sub], sls[h]]
                    oh = oh * lax.rsqrt(jnp.mean(oh * oh, axis=-1, keepdims=True) + EPS) * nw_ref[...]
                    oh = oh * _silu(z_ref[rsl[sub], sls[h]])
                o_ref[rsl[sub], sls[h]] = oh.astype(o_ref.dtype)
        for i, h in enumerate(hs):
            s_ref[h] = s_cur[i]

    groups = [head_group(list(range(g, g + GDN_GROUP_HEADS))) for g in range(0, nh, GDN_GROUP_HEADS)]
    live = [True] * len(groups)
    rnd = 0
    while any(live):
        for g, gen in enumerate(groups):
            if live[g] and rnd >= g:
                live[g] = next(gen, "done") != "done"
        rnd += 1


def _gdn_pass(qkv, ab, prm, *, rev, extra=None):
    bsz, l_len, _ = qkv.shape
    c = GDN_STEP_CHUNKS * GDN_CHUNK
    nc = l_len // c
    wd = GDN_WIDTH
    ch = (lambda n: nc - 1 - n) if rev else (lambda n: n)
    in_specs = [pl.BlockSpec((None, c, wd), lambda b, n: (b, ch(n), 0)),
                pl.BlockSpec((None, c, wd), lambda b, n: (b, ch(n), 1)),
                pl.BlockSpec((None, c, wd), lambda b, n: (b, ch(n), 2)),
                pl.BlockSpec((None, c, LANE), lambda b, n: (b, ch(n), 0)),
                pl.BlockSpec((8, LANE), lambda b, n: (0, 0))]
    args = [qkv, qkv, qkv, ab, prm]
    final = extra is not None
    if final:
        o_other, proj, z_col0, norm_w = extra
        zoff = z_col0 // wd
        in_specs += [pl.BlockSpec((None, c, wd), lambda b, n: (b, ch(n), 0)),
                     pl.BlockSpec((None, c, wd), lambda b, n: (b, ch(n), zoff)),
                     pl.BlockSpec((1, GDN_DK), lambda b, n: (0, 0))]
        args += [o_other, proj, norm_w.reshape(1, GDN_DK)]
    return pl.pallas_call(
        functools.partial(_gdn_kernel, rev=rev, final=final, nblk=nc),
        grid=(bsz, nc),
        in_specs=in_specs,
        out_specs=pl.BlockSpec((None, c, wd), lambda b, n: (b, ch(n), 0)),
        out_shape=jax.ShapeDtypeStruct((bsz, l_len, wd), BF16 if final else F32),
        scratch_shapes=[pltpu.VMEM((GDN_HEADS, GDN_DK, GDN_DK), F32)],
        compiler_params=_params("parallel", "arbitrary"),
        name="gdn_bwd" if rev else "gdn_fwd",
    )(*args)


def _diff_kernel(q_ref, k_ref, v_ref, lam_ref, nw_ref, o_ref, *, lam_init):
    key_ok = lax.broadcasted_iota(jnp.int32, (1, LEAD), 1) >= N_PAD
    v = v_ref[...]
    outs = []
    for m in range(2):
        sl = slice(m * DIFF_DQK, (m + 1) * DIFF_DQK)
        s = _dot_nt(q_ref[:, sl], k_ref[:, sl])
        s = jnp.concatenate([jnp.where(key_ok, s[:, :LEAD], NEG), s[:, LEAD:]], axis=1)
        e = jnp.exp2(s - jnp.max(s, axis=-1, keepdims=True))
        den = jnp.sum(e, axis=-1, keepdims=True)
        outs.append(_dot(e.astype(BF16), v) * (1.0 / den))
    lv = lam_ref[...]
    lam = (jnp.exp(jnp.sum(lv[0:1] * lv[1:2], axis=-1, keepdims=True))
           - jnp.exp(jnp.sum(lv[2:3] * lv[3:4], axis=-1, keepdims=True)) + lam_init)
    o = outs[0] - lam * outs[1]
    o = o * lax.rsqrt(jnp.mean(o * o, axis=-1, keepdims=True) + 1e-5) * nw_ref[...]
    o_ref[...] = (o * (1.0 - lam_init)).astype(o_ref.dtype)


def _diff_attention(qkv, lam_vecs, norm_w, lam_init, *, tq):
    bsz, l_len, _ = qkv.shape
    wq = 2 * DIFF_DQK
    nhd = DIFF_HEADS
    return pl.pallas_call(
        functools.partial(_diff_kernel, lam_init=lam_init),
        grid=(bsz, nhd, l_len // tq),
        in_specs=[pl.BlockSpec((None, tq, wq), lambda b, h, i: (b, i, h)),
                  pl.BlockSpec((None, l_len, wq), lambda b, h, i: (b, 0, nhd + h)),
                  pl.BlockSpec((None, l_len, DIFF_DV), lambda b, h, i: (b, 0, 2 * nhd + h)),
                  pl.BlockSpec((4, DIFF_DQK), lambda b, h, i: (0, 0)),
                  pl.BlockSpec((1, DIFF_DV), lambda b, h, i: (0, 0))],
        out_specs=pl.BlockSpec((None, tq, DIFF_DV), lambda b, h, i: (b, i, h)),
        out_shape=jax.ShapeDtypeStruct((bsz, l_len, DIFF_WIDTH), BF16),
        compiler_params=_params("parallel", "parallel", "parallel"),
        name="diff_attention",
    )(qkv, qkv, qkv, lam_vecs, norm_w.reshape(1, DIFF_DV))


def _swa_kernel(sink_ref, q_ref, k_ref, v_ref, o_ref):
    l_len = k_ref.shape[0]
    blk = SWA_BLOCK
    band = 3 * blk
    n = pl.program_id(1)
    start = pl.multiple_of(jnp.clip((n - 1) * blk, 0, l_len - band), blk)
    rows = SWA_GROUP * blk
    qpos = n * blk + (lax.broadcasted_iota(jnp.int32, (rows, band), 0) & (blk - 1))
    kpos = start + lax.broadcasted_iota(jnp.int32, (rows, band), 1)
    band_ok = (jnp.abs(qpos - kpos) <= SWA_WINDOW) & (kpos >= LEAD)
    meta_ok = lax.broadcasted_iota(jnp.int32, (rows, LEAD), 1) >= N_PAD
    kvs = range(SWA_KV_HEADS)
    hsl = [slice(h * HEAD_DIM, (h + 1) * HEAD_DIM) for h in kvs]
    q = [jnp.concatenate([q_ref[:, (h * SWA_GROUP + g) * HEAD_DIM:(h * SWA_GROUP + g + 1) * HEAD_DIM]
                          for g in range(SWA_GROUP)], axis=0) for h in kvs]
    sb = [jnp.where(band_ok, _dot_nt(q[h], k_ref[pl.ds(start, band), hsl[h]]), NEG) for h in kvs]
    sm = [jnp.where(meta_ok, _dot_nt(q[h], k_ref[0:LEAD, hsl[h]]), NEG) for h in kvs]
    sk = [jnp.concatenate([jnp.full((blk, 1), sink_ref[h * SWA_GROUP + g], F32) for g in range(SWA_GROUP)], axis=0)
          for h in kvs]
    mx = [jnp.maximum(jnp.maximum(jnp.max(sb[h], axis=-1, keepdims=True), jnp.max(sm[h], axis=-1, keepdims=True)),
                      sk[h]) for h in kvs]
    eb = [jnp.exp(sb[h] - mx[h]) for h in kvs]
    em = [jnp.exp(sm[h] - mx[h]) for h in kvs]
    den = [jnp.sum(eb[h], axis=-1, keepdims=True) + jnp.sum(em[h], axis=-1, keepdims=True) + jnp.exp(sk[h] - mx[h])
           for h in kvs]
    o = [(_dot(eb[h].astype(BF16), v_ref[pl.ds(start, band), hsl[h]])
          + _dot(em[h].astype(BF16), v_ref[0:LEAD, hsl[h]])) * (1.0 / den[h]) for h in kvs]
    for h in kvs:
        for g in range(SWA_GROUP):
            c0 = (h * SWA_GROUP + g) * HEAD_DIM
            o_ref[:, c0:c0 + HEAD_DIM] = o[h][g * blk:(g + 1) * blk].astype(o_ref.dtype)


def _window_attention(qkv, sink):
    bsz, l_len, _ = qkv.shape
    kv_w = SWA_KV_HEADS * HEAD_DIM
    koff = SWA_WIDTH // kv_w
    return pl.pallas_call(
        _swa_kernel,
        grid=(bsz, l_len // SWA_BLOCK),
        in_specs=[pl.BlockSpec(memory_space=pltpu.SMEM),
                  pl.BlockSpec((None, SWA_BLOCK, SWA_WIDTH), lambda b, i: (b, i, 0)),
                  pl.BlockSpec((None, l_len, kv_w), lambda b, i: (b, 0, koff)),
                  pl.BlockSpec((None, l_len, kv_w), lambda b, i: (b, 0, koff + 1))],
        out_specs=pl.BlockSpec((None, SWA_BLOCK, SWA_WIDTH), lambda b, i: (b, i, 0)),
        out_shape=jax.ShapeDtypeStruct((bsz, l_len, SWA_WIDTH), BF16),
        compiler_params=_params("parallel", "parallel"),
        name="window_attention",
    )(sink, qkv, qkv, qkv)


def _ssd_kernel(*refs, rev, final, nc):
    if final:
        x_ref, b_ref, c_ref, dt_ref, prm_ref, yb_ref, z0_ref, z1_ref, drow_ref, nw_ref, o_ref, s_ref = refs
    else:
        x_ref, b_ref, c_ref, dt_ref, prm_ref, o_ref, s_ref = refs
    q = SSD_CHUNK
    n = pl.program_id(1)

    @pl.when(n == 0)
    def _():
        s_ref[...] = jnp.zeros(s_ref.shape, F32)

    chunk = (nc - 1 - n) if rev else n
    rows = chunk * q + lax.broadcasted_iota(jnp.int32, (q, 1), 0)
    valid = rows >= N_PAD
    d = 1 if rev else 0
    p = SSD_HEADDIM
    gh = SSD_GROUP_HEADS
    gw = SSD_GROUP_WIDTH

    dt = jnp.where(valid, _softplus(dt_ref[...] + prm_ref[1:2, :]), 0.0)
    da = dt * (-jnp.exp(prm_ref[0:1, :]))
    ri = lax.broadcasted_iota(jnp.int32, (q, q), 0)
    ci = lax.broadcasted_iota(jnp.int32, (q, q), 1)
    incl = (ri <= ci) if rev else (ri >= ci)
    eye = (ri == ci).astype(F32)
    cum = _dot(incl.astype(F32), da, HIGHEST)
    cum_t = _dot_tn(cum, eye, HIGHEST)
    tot = cum[0:1, :] if rev else cum[q - 1:q, :]
    ecum = jnp.exp(cum)
    dstate = jnp.exp(tot - cum)
    etot = jnp.exp(tot)

    lane_lo = lax.broadcasted_iota(jnp.int32, (1, LANE), 1) < p
    groups = range(SSD_GROUPS)
    pairs = range(SSD_HEADS // 2)
    ppg = gh // 2
    col0 = [d * SSD_HEADS + 2 * j for j in pairs]

    def pair_cols(arr, j):
        return jnp.where(lane_lo, arr[:, col0[j]:col0[j] + 1], arr[:, col0[j] + 1:col0[j] + 2])

    def seg(col):
        return jnp.where(incl, jnp.exp(jnp.where(incl, cum[:, col:col + 1] - cum_t[col:col + 1, :], 0.0)), 0.0)

    bg = [b_ref[:, g * SSD_STATE:(g + 1) * SSD_STATE].astype(BF16) for g in groups]
    cg = [c_ref[:, g * SSD_STATE:(g + 1) * SSD_STATE].astype(BF16) for g in groups]
    sg = [s_ref[g * gw:(g + 1) * gw, :] for g in groups]
    cb = [_dot_nt(cg[g], bg[g]) for g in groups]
    y_off = [_dot_nt(cg[g], sg[g].astype(BF16)) for g in groups]
    xdt = [x_ref[:, j * LANE:(j + 1) * LANE] * pair_cols(dt, j) for j in pairs]
    lhs = [jnp.concatenate([cb[j // ppg] * seg(col0[j]), cb[j // ppg] * seg(col0[j] + 1)], axis=1) for j in pairs]
    rhs = [jnp.concatenate([jnp.where(lane_lo, xdt[j], 0.0), jnp.where(lane_lo, 0.0, xdt[j])], axis=0)
           for j in pairs]
    y_diag = [_bdot(lhs[j], rhs[j]) for j in pairs]
    ys = [y_diag[j] + y_off[j // ppg][:, (j % ppg) * LANE:(j % ppg + 1) * LANE] * pair_cols(ecum, j) for j in pairs]
    xs = [xdt[j] * pair_cols(dstate, j) for j in pairs]
    st = [_bdot_tn(jnp.concatenate(xs[g * ppg:(g + 1) * ppg], axis=1), bg[g]) for g in groups]
    for g in groups:
        decs = [jnp.broadcast_to(etot[:, c_:c_ + 1], (p, 1))
                for c_ in range(d * SSD_HEADS + g * gh, d * SSD_HEADS + (g + 1) * gh)]
        s_ref[g * gw:(g + 1) * gw, :] = sg[g] * jnp.concatenate(decs, axis=0) + st[g]
    y = jnp.concatenate(ys, axis=1)
    if final:
        y = y + yb_ref[...] + x_ref[...] * drow_ref[...]
        y = y * _silu(jnp.concatenate([z0_ref[...], z1_ref[...]], axis=1))
        for gi in range(SSD_GROUPS):
            sl = slice(gi * gw, (gi + 1) * gw)
            yg = y[:, sl]
            yg = yg * lax.rsqrt(jnp.mean(yg * yg, axis=-1, keepdims=True) + EPS) * nw_ref[:, sl]
            o_ref[:, sl] = yg.astype(o_ref.dtype)
    else:
        o_ref[...] = y


def _ssd_pass(xbc, dt_raw, prm, *, rev, extra=None):
    bsz, l_len, _ = xbc.shape
    q = SSD_CHUNK
    nc = l_len // q
    wd = SSD_WIDTH
    gs = SSD_GROUPS * SSD_STATE
    ch = (lambda n: nc - 1 - n) if rev else (lambda n: n)
    in_specs = [pl.BlockSpec((None, q, wd), lambda b, n: (b, ch(n), 0)),
                pl.BlockSpec((None, q, gs), lambda b, n: (b, ch(n), wd // gs)),
                pl.BlockSpec((None, q, gs), lambda b, n: (b, ch(n), wd // gs + 1)),
                pl.BlockSpec((None, q, LANE), lambda b, n: (b, ch(n), 0)),
                pl.BlockSpec((8, LANE), lambda b, n: (0, 0))]
    args = [xbc, xbc, xbc, dt_raw, prm]
    final = extra is not None
    if final:
        y_other, proj, z_col0, d_row, norm_w = extra
        zw = wd // 2
        zoff = z_col0 // zw
        in_specs += [pl.BlockSpec((None, q, wd), lambda b, n: (b, ch(n), 0)),
                     pl.BlockSpec((None, q, zw), lambda b, n: (b, ch(n), zoff)),
                     pl.BlockSpec((None, q, zw), lambda b, n: (b, ch(n), zoff + 1)),
                     pl.BlockSpec((1, wd), lambda b, n: (0, 0)),
                     pl.BlockSpec((1, wd), lambda b, n: (0, 0))]
        args += [y_other, proj, proj, d_row, norm_w.reshape(1, wd)]
    return pl.pallas_call(
        functools.partial(_ssd_kernel, rev=rev, final=final, nc=nc),
        grid=(bsz, nc),
        in_specs=in_specs,
        out_specs=pl.BlockSpec((None, q, wd), lambda b, n: (b, ch(n), 0)),
        out_shape=jax.ShapeDtypeStruct((bsz, l_len, wd), BF16 if final else F32),
        scratch_shapes=[pltpu.VMEM((wd, SSD_STATE), F32)],
        compiler_params=_params("parallel", "arbitrary"),
        name="ssd_bwd" if rev else "ssd_fwd",
    )(*args)


def _lane_params(*rows):
    out = jnp.zeros((8, LANE), F32)
    for i, r in enumerate(rows):
        r = r.astype(F32).reshape(-1)
        out = out.at[i, :r.shape[0]].set(r)
    return out


def _pad_cols(w, n):
    return jnp.pad(w, ((0, 0), (0, n - w.shape[1])))


def _diff_lambda_init(layer):
    return 0.8 - 0.6 * math.exp(-0.3 * layer)


def _tiles(l_pad):
    return _tile(l_pad, 1056, 16)


def _tall_tile(l_pad):
    return _tile(l_pad, 2112, 16)


def _even_mixer(h, hw, ssq, w_in, j, conv_w, a_log, dt_bias, gdn_norm_w, lam_vecs, diff_norm_w, w_out, lam_init,
                tables, next_norm_w):
    bsz, l_pad, d = h.shape
    tm = _tiles(l_pad)
    qkv_w = 3 * GDN_WIDTH
    main_w = qkv_w + GDN_WIDTH
    ab_w = 4 * GDN_HEADS
    wt = jnp.swapaxes(w_in, 1, 2)
    w_tail, w_ab = _tail_cast(wt, j, main_w, ab_w, wt.shape[1] - main_w - ab_w)
    tall = _tall_tile(l_pad)
    proj_a = _matmul([(hw, wt, j, 0, 0, True)], n=main_w, out_dtype=F32, tm=tall, tn=_tile(main_w, 512),
                     a_single=True, ssq=ssq, name="even_in_proj_a")
    qk_w = 2 * DIFF_HEADS * DIFF_DQK
    qkv_b = _matmul([(hw, w_tail, 0, 0, 0, True)], out_dtype=BF16, tm=tall, tn=_tile(w_tail.shape[1], 512),
                    a_single=True, ssq=ssq, rope=(tables, qk_w, qk_w, DIFF_DQK ** -0.5 * LOG2_E),
                    name="even_in_proj_b")
    ab = _matmul([(hw, w_ab, 0, 0, 0, True)], out_dtype=F32, tm=tm, tn=LANE, ssq=ssq, name="even_ab_proj")

    qkv_a = _conv_silu(proj_a, 0, qkv_w, conv_w, None)
    prm = _lane_params(a_log, dt_bias)
    o_bwd = _gdn_pass(qkv_a, ab, prm, rev=True)
    o_a = _gdn_pass(qkv_a, ab, prm, rev=False, extra=(o_bwd, proj_a, qkv_w, gdn_norm_w))

    o_b = _diff_attention(qkv_b, lam_vecs, diff_norm_w, lam_init, tq=_tile(l_pad, 384, 16))

    return _matmul([(o_a, w_out, j, 0, 0, False), (o_b, w_out, j, GDN_WIDTH, 0, False)], out_dtype=F32, tm=tm,
                   tn=_tile(d, 512), res=h, zero_pad=True, norm_w=next_norm_w, name="even_out_proj")


def _odd_mixer(h, hw, ssq, w_in, j, sink, conv_w, conv_b, a_log, dt_bias, d_skip, ssd_norm_w, w_out, tables,
               next_norm_w):
    bsz, l_pad, d = h.shape
    tm = _tiles(l_pad)
    kv_w = SWA_KV_HEADS * HEAD_DIM
    z_col = SWA_WIDTH + 2 * kv_w
    xbc_col = z_col + SSD_WIDTH
    main_w = xbc_col + SSD_XBC
    wt = jnp.swapaxes(w_in, 1, 2)
    w_dt = jnp.pad(wt[j, main_w:], ((0, LANE - (wt.shape[1] - main_w)), (0, 0))).astype(BF16)[None]
    tall = _tall_tile(l_pad)
    qkv_c = _matmul([(hw, wt, j, 0, 0, True)], n=z_col, out_dtype=BF16, tm=tall, tn=_tile(kv_w, 512),
                    a_single=True, ssq=ssq, rope=(tables, SWA_WIDTH, kv_w, HEAD_DIM ** -0.5), name="odd_in_proj_qkv")
    proj = _matmul([(hw, wt, j, 0, z_col, True)], n=main_w - z_col, out_dtype=F32, tm=tall,
                   tn=_tile(main_w - z_col, 512), a_single=True, ssq=ssq, name="odd_in_proj")
    dt_raw = _matmul([(hw, w_dt, 0, 0, 0, True)], out_dtype=F32, tm=tm, tn=LANE, ssq=ssq, name="odd_dt_proj")

    o_c = _window_attention(qkv_c, sink.astype(F32))

    xbc = _conv_silu(proj, xbc_col - z_col, SSD_XBC, conv_w, conv_b)
    prm = _lane_params(a_log, dt_bias)
    d_row = jnp.repeat(d_skip.astype(F32), SSD_HEADDIM).reshape(1, SSD_WIDTH)
    y_bwd = _ssd_pass(xbc, dt_raw, prm, rev=True)
    y = _ssd_pass(xbc, dt_raw, prm, rev=False, extra=(y_bwd, proj, 0, d_row, ssd_norm_w))

    return _matmul([(o_c, w_out, j, 0, 0, False), (y, w_out, j, SWA_WIDTH, 0, False)], out_dtype=F32, tm=tm,
                   tn=_tile(d, 512), res=h, zero_pad=True, norm_w=next_norm_w, name="odd_out_proj")


def _ffn(h, hw, ssq, w_gate, w_up, w_down_bf16, layer, next_norm_w):
    bsz, l_pad, d = h.shape
    f = w_gate.shape[2]
    act = _swiglu_up(hw, ssq, w_gate, w_up, layer, tm=_tall_tile(l_pad), tn=_tile(f, 256))
    return _matmul([(act, w_down_bf16, layer, 0, 0, False)], out_dtype=F32, tm=_tile(l_pad, 704, 16),
                   tn=_tile(d, 256), res=h, norm_w=next_norm_w, name="ffn_down")


def kernel(x, meta_tokens, norm_mix, norm_ffn, norm_final, even_w_in, even_conv, gdn_a_log, gdn_dt_bias, gdn_norm, diff_lam_q1, diff_lam_k1, diff_lam_q2, diff_lam_k2, diff_norm, even_w_out, odd_w_in, swa_sink, ssd_conv_w, ssd_conv_b, ssd_a_log, ssd_dt_bias, ssd_d, ssd_norm, odd_w_out, ffn_w_gate, ffn_w_up, ffn_w_down):
    bsz, seq, d = x.shape
    depth = norm_mix.shape[0]
    l_pad = LEAD + seq
    tables = _rope_tables(l_pad)
    w_down_bf16 = ffn_w_down.astype(BF16)
    h, hw, ssq = _embed_norm(x, meta_tokens, norm_mix[0])
    for i in range(depth):
        j = i // 2
        if i % 2 == 0:
            lam_vecs = jnp.stack([diff_lam_q1[j], diff_lam_k1[j], diff_lam_q2[j], diff_lam_k2[j]]).astype(F32)
            h, hw, ssq = _even_mixer(h, hw, ssq, even_w_in, j, even_conv[j], gdn_a_log[j], gdn_dt_bias[j],
                                     gdn_norm[j], lam_vecs, diff_norm[j], even_w_out, _diff_lambda_init(i), tables,
                                     norm_ffn[i])
        else:
            h, hw, ssq = _odd_mixer(h, hw, ssq, odd_w_in, j, swa_sink[j], ssd_conv_w[j], ssd_conv_b[j], ssd_a_log[j],
                                    ssd_dt_bias[j], ssd_d[j], ssd_norm[j], odd_w_out, tables, norm_ffn[i])
        if i + 1 < depth:
            h, hw, ssq = _ffn(h, hw, ssq, ffn_w_gate, ffn_w_up, w_down_bf16, i, norm_mix[i + 1])
        else:
            h = _ffn(h, hw, ssq, ffn_w_gate, ffn_w_up, w_down_bf16, i, None)
    return _rmsnorm(h, norm_final, out_dtype=x.dtype, row0=LEAD, rows=seq, tr=LANE)
```

```python
import functools
import math

import jax
import jax.numpy as jnp
from jax import lax
from jax.experimental import pallas as pl
from jax.experimental.pallas import tpu as pltpu

F32 = jnp.float32
BF16 = jnp.bfloat16
HIGHEST = lax.Precision.HIGHEST

N_META = 16
LEAD = 128
N_PAD = LEAD - N_META
HEAD_DIM = 128
ROT_DIM = HEAD_DIM // 4
ROPE_THETA = 500000.0
CONV_W = 7
EPS = 1e-6

GDN_HEADS = 16
GDN_DK = 128
GDN_WIDTH = 2048
GDN_CHUNK = 64
GDN_STEP_CHUNKS = 2
GDN_GROUP_HEADS = 4
DIFF_HEADS = 8
DIFF_DQK = 128
DIFF_DV = 256
DIFF_WIDTH = 2048
SWA_HEADS = 16
SWA_KV_HEADS = 4
SWA_GROUP = 4
SWA_WIDTH = 2048
SWA_WINDOW = 128
SWA_BLOCK = 128
SSD_WIDTH = 2048
SSD_HEADDIM = 64
SSD_HEADS = 32
SSD_STATE = 128
SSD_GROUPS = 4
SSD_GROUP_HEADS = SSD_HEADS // SSD_GROUPS
SSD_GROUP_WIDTH = SSD_WIDTH // SSD_GROUPS
SSD_XBC = 3072
SSD_CHUNK = 128

LANE = 128
NEG = -1e30
LOG2_E = math.log2(math.e)
VMEM_LIMIT = 56 * 1024 * 1024


def _params(*sem):
    return pltpu.CompilerParams(dimension_semantics=sem, vmem_limit_bytes=VMEM_LIMIT)


def _tile(n, pref, align=LANE):
    if n <= pref:
        return n
    t = (pref // align) * align
    while t >= align:
        if n % t == 0:
            return t
        t -= align
    return n


def _dot(a, b, precision=None):
    return jnp.dot(a, b, preferred_element_type=F32, precision=precision)


def _dot_nt(a, b, precision=None):
    return lax.dot_general(a, b, (((1,), (1,)), ((), ())), preferred_element_type=F32, precision=precision)


def _dot_tn(a, b, precision=None):
    return lax.dot_general(a, b, (((0,), (0,)), ((), ())), preferred_element_type=F32, precision=precision)


def _sigmoid(x):
    return 1.0 / (1.0 + jnp.exp(-x))


def _silu(x):
    return x * _sigmoid(x)


def _softplus(x):
    return jnp.maximum(x, 0.0) + jnp.log1p(jnp.exp(-jnp.abs(x)))


def _rmsnorm_kernel(x_ref, w_ref, o_ref, *, eps):
    x = x_ref[...]
    ms = jnp.mean(x * x, axis=-1, keepdims=True)
    o_ref[...] = (x * lax.rsqrt(ms + eps) * w_ref[...]).astype(o_ref.dtype)


def _rmsnorm(h, w, *, out_dtype, row0=0, rows=None, tr):
    bsz, l_len, d = h.shape
    rows = l_len - row0 if rows is None else rows
    off = row0 // tr
    return pl.pallas_call(
        functools.partial(_rmsnorm_kernel, eps=EPS),
        grid=(bsz, rows // tr),
        in_specs=[pl.BlockSpec((None, tr, d), lambda b, i: (b, i + off, 0)),
                  pl.BlockSpec((1, d), lambda b, i: (0, 0))],
        out_specs=pl.BlockSpec((None, tr, d), lambda b, i: (b, i, 0)),
        out_shape=jax.ShapeDtypeStruct((bsz, rows, d), out_dtype),
        compiler_params=_params("parallel", "parallel"),
        name="rmsnorm",
    )(h, w.reshape(1, d))


def _embed_kernel(x_ref, meta_ref, w_ref, h_ref, hw_ref, ssq_ref):
    def emit(blk):
        h_ref[...] = blk
        hw_ref[...] = (blk * w_ref[...]).astype(hw_ref.dtype)
        ssq_ref[...] = jnp.broadcast_to(jnp.sum(blk * blk, axis=-1, keepdims=True), ssq_ref.shape)

    i = pl.program_id(1)

    @pl.when(i == 0)
    def _():
        emit(jnp.concatenate([jnp.zeros((N_PAD, meta_ref.shape[-1]), F32), meta_ref[...]], axis=0))

    @pl.when(i > 0)
    def _():
        emit(x_ref[...])


def _embed_norm(x, meta, w):
    bsz, seq, d = x.shape
    l_pad = LEAD + seq
    blk = pl.BlockSpec((None, LEAD, d), lambda b, i: (b, i, 0))
    return pl.pallas_call(
        _embed_kernel,
        grid=(bsz, l_pad // LEAD),
        in_specs=[pl.BlockSpec((None, LEAD, d), lambda b, i: (b, jnp.maximum(i - 1, 0), 0)),
                  pl.BlockSpec((N_META, d), lambda b, i: (0, 0)),
                  pl.BlockSpec((1, d), lambda b, i: (0, 0))],
        out_specs=[blk, blk, pl.BlockSpec((None, LEAD, LANE), lambda b, i: (b, i, 0))],
        out_shape=[jax.ShapeDtypeStruct((bsz, l_pad, d), F32), jax.ShapeDtypeStruct((bsz, l_pad, d), BF16),
                   jax.ShapeDtypeStruct((bsz, l_pad, LANE), F32)],
        compiler_params=_params("parallel", "parallel"),
        name="embed_norm",
    )(x, meta.astype(F32), w.reshape(1, d))


def _tail_cast_kernel(x_ref, tail_ref, ab_ref, *, ab_w):
    tail_w, tk = tail_ref.shape
    tail_ref[...] = x_ref[ab_w:ab_w + tail_w, :].astype(tail_ref.dtype)
    ab = jnp.concatenate([x_ref[:ab_w, :], jnp.zeros((LANE - ab_w, tk), F32)], axis=0)
    ab_ref[...] = ab.astype(ab_ref.dtype)


def _tail_cast(wt, layer, row0, ab_w, tail_w):
    _, n_total, kdim = wt.shape
    win = row0
    assert row0 + ab_w + tail_w == n_total and ab_w + tail_w <= win
    tk = _tile(kdim, 256)
    return pl.pallas_call(
        functools.partial(_tail_cast_kernel, ab_w=ab_w),
        grid=(kdim // tk,),
        in_specs=[pl.BlockSpec((None, win, tk), lambda k: (layer, 1, k))],
        out_specs=[pl.BlockSpec((None, tail_w, tk), lambda k: (0, 0, k)),
                   pl.BlockSpec((None, LANE, tk), lambda k: (0, 0, k))],
        out_shape=[jax.ShapeDtypeStruct((1, tail_w, kdim), BF16), jax.ShapeDtypeStruct((1, LANE, kdim), BF16)],
        compiler_params=_params("parallel"),
        name="tail_cast",
    )(wt)


def _inv_rms(ssq_ref, d_model):
    return lax.rsqrt(ssq_ref[:, 0:1] * (1.0 / d_model) + EPS)


def _mm_kernel(*refs, trans, has_res, has_scale, zero_pad, emit_norm, tm, d_model):
    n_pairs = len(trans)
    ins = refs[:2 * n_pairs]
    pos = 2 * n_pairs
    res_ref = refs[pos] if has_res else None
    pos += int(has_res)
    ssq_in_ref = refs[pos] if has_scale else None
    pos += int(has_scale)
    nw_ref = refs[pos] if emit_norm else None
    pos += int(emit_norm)
    o_ref = refs[pos]

    acc = None
    for p in range(n_pairs):
        w = ins[2 * p + 1][...].astype(BF16)
        d = _dot_nt(ins[2 * p][...], w) if trans[p] else _dot(ins[2 * p][...], w)
        acc = d if acc is None else acc + d
    if has_scale:
        acc = acc * _inv_rms(ssq_in_ref, d_model)
    if has_res:
        acc = acc + res_ref[...]
    if zero_pad:
        row = pl.program_id(1) * tm + lax.broadcasted_iota(jnp.int32, acc.shape, 0)
        acc = jnp.where(row < N_PAD, 0.0, acc)
    o_ref[...] = acc.astype(o_ref.dtype)
    if emit_norm:
        hw_ref, ssq_ref = refs[pos + 1], refs[pos + 2]
        hw_ref[...] = (acc * nw_ref[...]).astype(hw_ref.dtype)
        part = jnp.broadcast_to(jnp.sum(acc * acc, axis=-1, keepdims=True), ssq_ref.shape)
        j = pl.program_id(2)

        @pl.when(j == 0)
        def _():
            ssq_ref[...] = part

        @pl.when(j > 0)
        def _():
            ssq_ref[...] += part


def _row_spec(block, index_map, single_buffer):
    if single_buffer:
        return pl.BlockSpec(block, index_map, pipeline_mode=pl.Buffered(1))
    return pl.BlockSpec(block, index_map)


def _matmul(pairs, *, out_dtype, tm, tn, n=None, res=None, zero_pad=False, a_single=False, ssq=None, norm_w=None,
            name):
    bsz, l_len, kdim = pairs[0][0].shape
    if n is None:
        n = pairs[0][1].shape[1 if pairs[0][5] else 2]
    in_specs, args = [], []
    for a, w, layer, row0, col0, trans in pairs:
        assert row0 % kdim == 0 and col0 % tn == 0
        r, c = row0 // kdim, col0 // tn
        w_spec = (pl.BlockSpec((None, tn, kdim), lambda b_, i, j, ly=layer, r=r, c=c: (ly, j + c, r)) if trans else
                  pl.BlockSpec((None, kdim, tn), lambda b_, i, j, ly=layer, r=r, c=c: (ly, r, j + c)))
        in_specs += [_row_spec((None, tm, kdim), lambda b_, i, j: (b_, i, 0), a_single), w_spec]
        args += [a, w]
    tile = pl.BlockSpec((None, tm, tn), lambda b_, i, j: (b_, i, j))
    stat = pl.BlockSpec((None, tm, LANE), lambda b_, i, j: (b_, i, 0))
    if res is not None:
        in_specs.append(tile)
        args.append(res)
    if ssq is not None:
        in_specs.append(stat)
        args.append(ssq)
    out_specs, out_shape = tile, jax.ShapeDtypeStruct((bsz, l_len, n), out_dtype)
    if norm_w is not None:
        in_specs.append(pl.BlockSpec((1, tn), lambda b_, i, j: (0, j)))
        args.append(norm_w.reshape(1, n))
        out_specs = [tile, tile, stat]
        out_shape = [out_shape, jax.ShapeDtypeStruct((bsz, l_len, n), BF16),
                     jax.ShapeDtypeStruct((bsz, l_len, LANE), F32)]
    return pl.pallas_call(
        functools.partial(_mm_kernel, trans=tuple(p[5] for p in pairs), has_res=res is not None,
                          has_scale=ssq is not None, zero_pad=zero_pad, emit_norm=norm_w is not None, tm=tm,
                          d_model=kdim),
        grid=(bsz, l_len // tm, n // tn),
        in_specs=in_specs,
        out_specs=out_specs,
        out_shape=out_shape,
        compiler_params=_params("parallel", "parallel", "arbitrary"),
        name=name,
    )(*args)


def _swiglu_kernel(a_ref, g_ref, u_ref, ssq_ref, o_ref):
    a = a_ref[...]
    r = _inv_rms(ssq_ref, a.shape[-1])
    g = _dot(a, g_ref[...].astype(BF16)) * r
    u = _dot(a, u_ref[...].astype(BF16)) * r
    o_ref[...] = (_silu(g) * u).astype(o_ref.dtype)


def _swiglu_up(a, ssq, w_gate, w_up, layer, *, tm, tn):
    bsz, l_len, kdim = a.shape
    n = w_gate.shape[2]
    return pl.pallas_call(
        _swiglu_kernel,
        grid=(bsz, l_len // tm, n // tn),
        in_specs=[_row_spec((None, tm, kdim), lambda b, i, j: (b, i, 0), True),
                  pl.BlockSpec((None, kdim, tn), lambda b, i, j: (layer, 0, j)),
                  pl.BlockSpec((None, kdim, tn), lambda b, i, j: (layer, 0, j)),
                  pl.BlockSpec((None, tm, LANE), lambda b, i, j: (b, i, 0))],
        out_specs=pl.BlockSpec((None, tm, tn), lambda b, i, j: (b, i, j)),
        out_shape=jax.ShapeDtypeStruct((bsz, l_len, n), BF16),
        compiler_params=_params("parallel", "parallel", "parallel"),
        name="swiglu_up",
    )(a, w_gate, w_up, ssq)


def _conv_kernel(*refs, l_len, rc, has_bias):
    if has_bias:
        x_ref, w_ref, b_ref, o_ref, xp_ref = refs
    else:
        x_ref, w_ref, o_ref, xp_ref = refs
        b_ref = None
    tc = x_ref.shape[-1]
    halo = 8
    xp_ref[0:halo, :] = jnp.zeros((halo, tc), F32)
    xp_ref[l_len + halo:l_len + 2 * halo, :] = jnp.zeros((halo, tc), F32)
    xp_ref[halo:l_len + halo, :] = x_ref[...]
    w = w_ref[...]
    reach = CONV_W // 2
    for c in range(l_len // rc):
        r0 = c * rc
        acc = None
        xh = xp_ref[r0:r0 + rc + 2 * halo, :]
        for j in range(CONV_W):
            off = halo - reach + j
            tap = xh[halo:halo + rc] if off == halo else pltpu.roll(xh, rc + 2 * halo - off, 0)[:rc]
            t = tap * w[j:j + 1, :]
            acc = t if acc is None else acc + t
        if has_bias:
            acc = acc + b_ref[...]
        y = _silu(acc)
        if r0 < N_PAD:
            row = r0 + lax.broadcasted_iota(jnp.int32, y.shape, 0)
            y = jnp.where(row < N_PAD, 0.0, y)
        o_ref[r0:r0 + rc, :] = y


def _conv_silu(proj, col0, width, w, bias, *, tc=256, rc=128):
    bsz, l_len, _ = proj.shape
    off = col0 // tc
    in_specs = [pl.BlockSpec((None, l_len, tc), lambda b, j: (b, 0, j + off)),
                pl.BlockSpec((CONV_W, tc), lambda b, j: (0, j))]
    args = [proj, w]
    if bias is not None:
        in_specs.append(pl.BlockSpec((1, tc), lambda b, j: (0, j)))
        args.append(bias.reshape(1, width))
    return pl.pallas_call(
        functools.partial(_conv_kernel, l_len=l_len, rc=rc, has_bias=bias is not None),
        grid=(bsz, width // tc),
        in_specs=in_specs,
        out_specs=pl.BlockSpec((None, l_len, tc), lambda b, j: (b, 0, j)),
        out_shape=jax.ShapeDtypeStruct((bsz, l_len, width), F32),
        scratch_shapes=[pltpu.VMEM((l_len + 16, tc), F32)],
        compiler_params=_params("parallel", "parallel"),
        name="conv_silu",
    )(*args)


def _rope(x, cf, s1, s2):
    half = ROT_DIM // 2
    return x * cf + pltpu.roll(x, LANE - half, 1) * s1 + pltpu.roll(x, half, 1) * s2


def _rope_cast_kernel(q_ref, k_ref, v_ref, cf_ref, s1_ref, s2_ref, o_ref, *, q_scale):
    cf, s1, s2 = cf_ref[...], s1_ref[...], s2_ref[...]
    wq, wk, wv = q_ref.shape[-1], k_ref.shape[-1], v_ref.shape[-1]
    for hd in range(wq // HEAD_DIM):
        sl = slice(hd * HEAD_DIM, (hd + 1) * HEAD_DIM)
        o_ref[:, sl] = (_rope(q_ref[:, sl], cf, s1, s2) * q_scale).astype(o_ref.dtype)
    for hd in range(wk // HEAD_DIM):
        sl = slice(hd * HEAD_DIM, (hd + 1) * HEAD_DIM)
        o_ref[:, wq + hd * HEAD_DIM:wq + (hd + 1) * HEAD_DIM] = _rope(k_ref[:, sl], cf, s1, s2).astype(o_ref.dtype)
    o_ref[:, wq + wk:wq + wk + wv] = v_ref[...].astype(o_ref.dtype)


def _rope_cast(proj, cols, tables, *, tr, q_scale):
    bsz, l_len, _ = proj.shape
    width = sum(w for _, w in cols)
    tab_spec = pl.BlockSpec((tr, LANE), lambda b, i: (i, 0))

    def col_spec(col0, w):
        return pl.BlockSpec((None, tr, w), lambda b, i: (b, i, col0 // w))

    return pl.pallas_call(
        functools.partial(_rope_cast_kernel, q_scale=q_scale),
        grid=(bsz, l_len // tr),
        in_specs=[col_spec(*c) for c in cols] + [tab_spec, tab_spec, tab_spec],
        out_specs=pl.BlockSpec((None, tr, width), lambda b, i: (b, i, 0)),
        out_shape=jax.ShapeDtypeStruct((bsz, l_len, width), BF16),
        compiler_params=_params("parallel", "parallel"),
        name="rope_cast",
    )(proj, proj, proj, *tables)


def _rope_tables(l_pad):
    half = ROT_DIM // 2
    inv = ROPE_THETA ** (-jnp.arange(0, ROT_DIM, 2, dtype=F32) / ROT_DIM)
    pos = (jnp.arange(l_pad) - N_PAD).astype(F32)
    ang = pos[:, None] * inv[None, :]
    cos, sin = jnp.cos(ang), jnp.sin(ang)
    ones = jnp.ones((l_pad, LANE - ROT_DIM), F32)
    zeros = jnp.zeros((l_pad, LANE - ROT_DIM), F32)
    zh = jnp.zeros((l_pad, half), F32)
    cf = jnp.concatenate([cos, cos, ones], axis=1)
    s1 = jnp.concatenate([-sin, zh, zeros], axis=1)
    s2 = jnp.concatenate([zh, sin, zeros], axis=1)
    return cf, s1, s2


def _bdot(a, b):
    return _dot(a.astype(BF16), b.astype(BF16))


def _bdot_nt(a, b):
    return _dot_nt(a.astype(BF16), b.astype(BF16))


def _bdot_tn(a, b):
    return _dot_tn(a.astype(BF16), b.astype(BF16))


def _gdn_kernel(*refs, rev, final, nblk):
    if final:
        q_ref, k_ref, v_ref, ab_ref, prm_ref, ob_ref, z_ref, nw_ref, o_ref, s_ref = refs
    else:
        q_ref, k_ref, v_ref, ab_ref, prm_ref, o_ref, s_ref = refs
    c = GDN_CHUNK
    nsub = q_ref.shape[0] // c
    n = pl.program_id(1)

    @pl.when(n == 0)
    def _():
        s_ref[...] = jnp.zeros(s_ref.shape, F32)

    blk = (nblk - 1 - n) if rev else n
    d = 1 if rev else 0
    nh = GDN_HEADS
    heads = range(nh)
    sls = [slice(h * GDN_DK, (h + 1) * GDN_DK) for h in heads]
    cols = [d * nh + h for h in heads]

    ri = lax.broadcasted_iota(jnp.int32, (c, c), 0)
    ci = lax.broadcasted_iota(jnp.int32, (c, c), 1)
    incl = (ri <= ci) if rev else (ri >= ci)
    strict = (ri < ci) if rev else (ri > ci)
    eye = (ri == ci).astype(F32)

    rsl, cums, cum_ts, tots, betas = [], [], [], [], []
    for sub in range(nsub):
        rs = slice(sub * c, (sub + 1) * c)
        rows = (blk * nsub + sub) * c + lax.broadcasted_iota(jnp.int32, (c, 1), 0)
        valid = rows >= N_PAD
        ab = ab_ref[rs, :]
        g_all = -jnp.exp(prm_ref[0:1, :]) * _softplus(ab + prm_ref[1:2, :])
        g_all = jnp.where(valid, g_all, 0.0)
        cum = _dot(incl.astype(F32), g_all, HIGHEST)
        rsl.append(rs)
        cums.append(cum)
        cum_ts.append(_dot_tn(cum, eye, HIGHEST))
        tots.append(cum[0:1, :] if rev else cum[c - 1:c, :])
        betas.append(jnp.where(valid, _sigmoid(ab), 0.0))
    n_sq = int(math.log2(c)) - 1
    scan_order = range(nsub - 1, -1, -1) if rev else range(nsub)

    def head_group(hs):
        units = [(sub, h) for sub in range(nsub) for h in hs]
        ids = range(len(units))
        gcol = [cums[sub][:, cols[h]:cols[h] + 1] for sub, h in units]
        tot_h = [tots[sub][:, cols[h]:cols[h] + 1] for sub, h in units]
        bcol = [betas[sub][:, 2 * nh + cols[h]:2 * nh + cols[h] + 1] for sub, h in units]
        egc = [jnp.exp(g_) for g_ in gcol]
        qn, kn = [], []
        for sub, h in units:
            qh, kh = q_ref[rsl[sub], sls[h]], k_ref[rsl[sub], sls[h]]
            qn.append(qh * lax.rsqrt(jnp.sum(qh * qh, axis=-1, keepdims=True) + 1e-6) * (GDN_DK ** -0.5))
            kn.append(kh * lax.rsqrt(jnp.sum(kh * kh, axis=-1, keepdims=True) + 1e-6))
        kb = [kn[u] * bcol[u] for u in ids]
        vb = [v_ref[rsl[sub], sls[h]] * bcol[u] for u, (sub, h) in enumerate(units)]
        decay = [jnp.where(incl, jnp.exp(jnp.where(incl, gcol[u] - cum_ts[sub][cols[h]:cols[h] + 1, :], 0.0)), 0.0)
                 for u, (sub, h) in enumerate(units)]
        yield
        kq = [_bdot_nt(jnp.concatenate([kb[u], qn[u]], axis=0), kn[u]) for u in ids]
        yield
        lower = [jnp.where(strict, kq[u][:c] * decay[u], 0.0) for u in ids]
        qk = [kq[u][c:] * decay[u] for u in ids]
        x = [eye - lower[u] for u in ids]
        p = [_bdot(lower[u], lower[u]) for u in ids]
        yield
        for t in range(n_sq):
            x = [x[u] + _bdot(x[u], p[u]) for u in ids]
            if t + 1 < n_sq:
                p = [_bdot(p[u], p[u]) for u in ids]
            yield
        uw = [_bdot(x[u], jnp.concatenate([vb[u], kb[u] * egc[u]], axis=1)) for u in ids]
        yield
        s_cur = [s_ref[h] for h in hs]
        for sub in scan_order:
            us = [sub * len(hs) + i for i in range(len(hs))]
            wq = [_bdot(jnp.concatenate([uw[u][:, GDN_DK:], qn[u] * egc[u]], axis=0), s_cur[i])
                  for i, u in enumerate(us)]
            yield
            v_new = [uw[u][:, :GDN_DK] - wq[i][:c] for i, u in enumerate(us)]
            o = [wq[i][c:] + _bdot(qk[u], v_new[i]) for i, u in enumerate(us)]
            kv = [_bdot_tn(kn[u] * jnp.exp(tot_h[u] - gcol[u]), v_new[i]) for i, u in enumerate(us)]
            yield
            s_cur = [s_cur[i] * jnp.exp(tot_h[u]) + kv[i] for i, u in enumerate(us)]
            for i, h in enumerate(hs):
                oh = o[i]
                if final:
                    oh = oh + ob_ref[rsl[sub], sls[h]]
                    oh = oh * lax.rsqrt(jnp.mean(oh * oh, axis=-1, keepdims=True) + EPS) * nw_ref[...]
                    oh = oh * _silu(z_ref[rsl[sub], sls[h]])
                o_ref[rsl[sub], sls[h]] = oh.astype(o_ref.dtype)
        for i, h in enumerate(hs):
            s_ref[h] = s_cur[i]

    groups = [head_group(list(range(g, g + GDN_GROUP_HEADS))) for g in range(0, nh, GDN_GROUP_HEADS)]
    live = [True] * len(groups)
    rnd = 0
    while any(live):
        for g, gen in enumerate(groups):
            if live[g] and rnd >= g:
                live[g] = next(gen, "done") != "done"
        rnd += 1


def _gdn_pass(qkv, ab, prm, *, rev, extra=None):
    bsz, l_len, _ = qkv.shape
    c = GDN_STEP_CHUNKS * GDN_CHUNK
    nc = l_len // c
    wd = GDN_WIDTH
    ch = (lambda n: nc - 1 - n) if rev else (lambda n: n)
    in_specs = [pl.BlockSpec((None, c, wd), lambda b, n: (b, ch(n), 0)),
                pl.BlockSpec((None, c, wd), lambda b, n: (b, ch(n), 1)),
                pl.BlockSpec((None, c, wd), lambda b, n: (b, ch(n), 2)),
                pl.BlockSpec((None, c, LANE), lambda b, n: (b, ch(n), 0)),
                pl.BlockSpec((8, LANE), lambda b, n: (0, 0))]
    args = [qkv, qkv, qkv, ab, prm]
    final = extra is not None
    if final:
        o_other, proj, z_col0, norm_w = extra
        zoff = z_col0 // wd
        in_specs += [pl.BlockSpec((None, c, wd), lambda b, n: (b, ch(n), 0)),
                     pl.BlockSpec((None, c, wd), lambda b, n: (b, ch(n), zoff)),
                     pl.BlockSpec((1, GDN_DK), lambda b, n: (0, 0))]
        args += [o_other, proj, norm_w.reshape(1, GDN_DK)]
    return pl.pallas_call(
        functools.partial(_gdn_kernel, rev=rev, final=final, nblk=nc),
        grid=(bsz, nc),
        in_specs=in_specs,
        out_specs=pl.BlockSpec((None, c, wd), lambda b, n: (b, ch(n), 0)),
        out_shape=jax.ShapeDtypeStruct((bsz, l_len, wd), BF16 if final else F32),
        scratch_shapes=[pltpu.VMEM((GDN_HEADS, GDN_DK, GDN_DK), F32)],
        compiler_params=_params("parallel", "arbitrary"),
        name="gdn_bwd" if rev else "gdn_fwd",
    )(*args)


def _diff_kernel(q_ref, k_ref, v_ref, lam_ref, nw_ref, o_ref, *, lam_init):
    key_ok = lax.broadcasted_iota(jnp.int32, (1, LEAD), 1) >= N_PAD
    v = v_ref[...]
    outs = []
    for m in range(2):
        sl = slice(m * DIFF_DQK, (m + 1) * DIFF_DQK)
        s = _dot_nt(q_ref[:, sl], k_ref[:, sl])
        s = jnp.concatenate([jnp.where(key_ok, s[:, :LEAD], NEG), s[:, LEAD:]], axis=1)
        e = jnp.exp2(s - jnp.max(s, axis=-1, keepdims=True))
        den = jnp.sum(e, axis=-1, keepdims=True)
        outs.append(_dot(e.astype(BF16), v) * (1.0 / den))
    lv = lam_ref[...]
    lam = (jnp.exp(jnp.sum(lv[0:1] * lv[1:2], axis=-1, keepdims=True))
           - jnp.exp(jnp.sum(lv[2:3] * lv[3:4], axis=-1, keepdims=True)) + lam_init)
    o = outs[0] - lam * outs[1]
    o = o * lax.rsqrt(jnp.mean(o * o, axis=-1, keepdims=True) + 1e-5) * nw_ref[...]
    o_ref[...] = (o * (1.0 - lam_init)).astype(o_ref.dtype)


def _diff_attention(qkv, lam_vecs, norm_w, lam_init, *, tq):
    bsz, l_len, _ = qkv.shape
    wq = 2 * DIFF_DQK
    nhd = DIFF_HEADS
    return pl.pallas_call(
        functools.partial(_diff_kernel, lam_init=lam_init),
        grid=(bsz, nhd, l_len // tq),
        in_specs=[pl.BlockSpec((None, tq, wq), lambda b, h, i: (b, i, h)),
                  pl.BlockSpec((None, l_len, wq), lambda b, h, i: (b, 0, nhd + h)),
                  pl.BlockSpec((None, l_len, DIFF_DV), lambda b, h, i: (b, 0, 2 * nhd + h)),
                  pl.BlockSpec((4, DIFF_DQK), lambda b, h, i: (0, 0)),
                  pl.BlockSpec((1, DIFF_DV), lambda b, h, i: (0, 0))],
        out_specs=pl.BlockSpec((None, tq, DIFF_DV), lambda b, h, i: (b, i, h)),
        out_shape=jax.ShapeDtypeStruct((bsz, l_len, DIFF_WIDTH), BF16),
        compiler_params=_params("parallel", "parallel", "parallel"),
        name="diff_attention",
    )(qkv, qkv, qkv, lam_vecs, norm_w.reshape(1, DIFF_DV))


def _swa_kernel(sink_ref, q_ref, k_ref, v_ref, o_ref):
    l_len = k_ref.shape[0]
    blk = SWA_BLOCK
    band = 3 * blk
    n = pl.program_id(1)
    start = pl.multiple_of(jnp.clip((n - 1) * blk, 0, l_len - band), blk)
    rows = SWA_GROUP * blk
    qpos = n * blk + (lax.broadcasted_iota(jnp.int32, (rows, band), 0) & (blk - 1))
    kpos = start + lax.broadcasted_iota(jnp.int32, (rows, band), 1)
    band_ok = (jnp.abs(qpos - kpos) <= SWA_WINDOW) & (kpos >= LEAD)
    meta_ok = lax.broadcasted_iota(jnp.int32, (rows, LEAD), 1) >= N_PAD
    kvs = range(SWA_KV_HEADS)
    hsl = [slice(h * HEAD_DIM, (h + 1) * HEAD_DIM) for h in kvs]
    q = [jnp.concatenate([q_ref[:, (h * SWA_GROUP + g) * HEAD_DIM:(h * SWA_GROUP + g + 1) * HEAD_DIM]
                          for g in range(SWA_GROUP)], axis=0) for h in kvs]
    sb = [jnp.where(band_ok, _dot_nt(q[h], k_ref[pl.ds(start, band), hsl[h]]), NEG) for h in kvs]
    sm = [jnp.where(meta_ok, _dot_nt(q[h], k_ref[0:LEAD, hsl[h]]), NEG) for h in kvs]
    sk = [jnp.concatenate([jnp.full((blk, 1), sink_ref[h * SWA_GROUP + g], F32) for g in range(SWA_GROUP)], axis=0)
          for h in kvs]
    mx = [jnp.maximum(jnp.maximum(jnp.max(sb[h], axis=-1, keepdims=True), jnp.max(sm[h], axis=-1, keepdims=True)),
                      sk[h]) for h in kvs]
    eb = [jnp.exp(sb[h] - mx[h]) for h in kvs]
    em = [jnp.exp(sm[h] - mx[h]) for h in kvs]
    den = [jnp.sum(eb[h], axis=-1, keepdims=True) + jnp.sum(em[h], axis=-1, keepdims=True) + jnp.exp(sk[h] - mx[h])
           for h in kvs]
    o = [(_dot(eb[h].astype(BF16), v_ref[pl.ds(start, band), hsl[h]])
          + _dot(em[h].astype(BF16), v_ref[0:LEAD, hsl[h]])) * (1.0 / den[h]) for h in kvs]
    for h in kvs:
        for g in range(SWA_GROUP):
            c0 = (h * SWA_GROUP + g) * HEAD_DIM
            o_ref[:, c0:c0 + HEAD_DIM] = o[h][g * blk:(g + 1) * blk].astype(o_ref.dtype)


def _window_attention(qkv, sink):
    bsz, l_len, _ = qkv.shape
    kv_w = SWA_KV_HEADS * HEAD_DIM
    koff = SWA_WIDTH // kv_w
    return pl.pallas_call(
        _swa_kernel,
        grid=(bsz, l_len // SWA_BLOCK),
        in_specs=[pl.BlockSpec(memory_space=pltpu.SMEM),
                  pl.BlockSpec((None, SWA_BLOCK, SWA_WIDTH), lambda b, i: (b, i, 0)),
                  pl.BlockSpec((None, l_len, kv_w), lambda b, i: (b, 0, koff)),
                  pl.BlockSpec((None, l_len, kv_w), lambda b, i: (b, 0, koff + 1))],
        out_specs=pl.BlockSpec((None, SWA_BLOCK, SWA_WIDTH), lambda b, i: (b, i, 0)),
        out_shape=jax.ShapeDtypeStruct((bsz, l_len, SWA_WIDTH), BF16),
        compiler_params=_params("parallel", "parallel"),
        name="window_attention",
    )(sink, qkv, qkv, qkv)


def _ssd_kernel(*refs, rev, final, nc):
    if final:
        x_ref, b_ref, c_ref, dt_ref, prm_ref, yb_ref, z0_ref, z1_ref, drow_ref, nw_ref, o_ref, s_ref = refs
    else:
        x_ref, b_ref, c_ref, dt_ref, prm_ref, o_ref, s_ref = refs
    q = SSD_CHUNK
    n = pl.program_id(1)

    @pl.when(n == 0)
    def _():
        s_ref[...] = jnp.zeros(s_ref.shape, F32)

    chunk = (nc - 1 - n) if rev else n
    rows = chunk * q + lax.broadcasted_iota(jnp.int32, (q, 1), 0)
    valid = rows >= N_PAD
    d = 1 if rev else 0
    p = SSD_HEADDIM
    gh = SSD_GROUP_HEADS
    gw = SSD_GROUP_WIDTH

    dt = jnp.where(valid, _softplus(dt_ref[...] + prm_ref[1:2, :]), 0.0)
    da = dt * (-jnp.exp(prm_ref[0:1, :]))
    ri = lax.broadcasted_iota(jnp.int32, (q, q), 0)
    ci = lax.broadcasted_iota(jnp.int32, (q, q), 1)
    incl = (ri <= ci) if rev else (ri >= ci)
    eye = (ri == ci).astype(F32)
    cum = _dot(incl.astype(F32), da, HIGHEST)
    cum_t = _dot_tn(cum, eye, HIGHEST)
    tot = cum[0:1, :] if rev else cum[q - 1:q, :]
    ecum = jnp.exp(cum)
    dstate = jnp.exp(tot - cum)
    etot = jnp.exp(tot)

    lane_lo = lax.broadcasted_iota(jnp.int32, (1, LANE), 1) < p
    groups = range(SSD_GROUPS)
    pairs = range(SSD_HEADS // 2)
    ppg = gh // 2
    col0 = [d * SSD_HEADS + 2 * j for j in pairs]

    def pair_cols(arr, j):
        return jnp.where(lane_lo, arr[:, col0[j]:col0[j] + 1], arr[:, col0[j] + 1:col0[j] + 2])

    def seg(col):
        return jnp.where(incl, jnp.exp(jnp.where(incl, cum[:, col:col + 1] - cum_t[col:col + 1, :], 0.0)), 0.0)

    bg = [b_ref[:, g * SSD_STATE:(g + 1) * SSD_STATE].astype(BF16) for g in groups]
    cg = [c_ref[:, g * SSD_STATE:(g + 1) * SSD_STATE].astype(BF16) for g in groups]
    sg = [s_ref[g * gw:(g + 1) * gw, :] for g in groups]
    cb = [_dot_nt(cg[g], bg[g]) for g in groups]
    y_off = [_dot_nt(cg[g], sg[g].astype(BF16)) for g in groups]
    xdt = [x_ref[:, j * LANE:(j + 1) * LANE] * pair_cols(dt, j) for j in pairs]
    lhs = [jnp.concatenate([cb[j // ppg] * seg(col0[j]), cb[j // ppg] * seg(col0[j] + 1)], axis=1) for j in pairs]
    rhs = [jnp.concatenate([jnp.where(lane_lo, xdt[j], 0.0), jnp.where(lane_lo, 0.0, xdt[j])], axis=0)
           for j in pairs]
    y_diag = [_bdot(lhs[j], rhs[j]) for j in pairs]
    ys = [y_diag[j] + y_off[j // ppg][:, (j % ppg) * LANE:(j % ppg + 1) * LANE] * pair_cols(ecum, j) for j in pairs]
    xs = [xdt[j] * pair_cols(dstate, j) for j in pairs]
    st = [_bdot_tn(jnp.concatenate(xs[g * ppg:(g + 1) * ppg], axis=1), bg[g]) for g in groups]
    for g in groups:
        decs = [jnp.broadcast_to(etot[:, c_:c_ + 1], (p, 1))
                for c_ in range(d * SSD_HEADS + g * gh, d * SSD_HEADS + (g + 1) * gh)]
        s_ref[g * gw:(g + 1) * gw, :] = sg[g] * jnp.concatenate(decs, axis=0) + st[g]
    y = jnp.concatenate(ys, axis=1)
    if final:
        y = y + yb_ref[...] + x_ref[...] * drow_ref[...]
        y = y * _silu(jnp.concatenate([z0_ref[...], z1_ref[...]], axis=1))
        for gi in range(SSD_GROUPS):
            sl = slice(gi * gw, (gi + 1) * gw)
            yg = y[:, sl]
            yg = yg * lax.rsqrt(jnp.mean(yg * yg, axis=-1, keepdims=True) + EPS) * nw_ref[:, sl]
            o_ref[:, sl] = yg.astype(o_ref.dtype)
    else:
        o_ref[...] = y


def _ssd_pass(xbc, dt_raw, prm, *, rev, extra=None):
    bsz, l_len, _ = xbc.shape
    q = SSD_CHUNK
    nc = l_len // q
    wd = SSD_WIDTH
    gs = SSD_GROUPS * SSD_STATE
    ch = (lambda n: nc - 1 - n) if rev else (lambda n: n)
    in_specs = [pl.BlockSpec((None, q, wd), lambda b, n: (b, ch(n), 0)),
                pl.BlockSpec((None, q, gs), lambda b, n: (b, ch(n), wd // gs)),
                pl.BlockSpec((None, q, gs), lambda b, n: (b, ch(n), wd // gs + 1)),
                pl.BlockSpec((None, q, LANE), lambda b, n: (b, ch(n), 0)),
                pl.BlockSpec((8, LANE), lambda b, n: (0, 0))]
    args = [xbc, xbc, xbc, dt_raw, prm]
    final = extra is not None
    if final:
        y_other, proj, z_col0, d_row, norm_w = extra
        zw = wd // 2
        zoff = z_col0 // zw
        in_specs += [pl.BlockSpec((None, q, wd), lambda b, n: (b, ch(n), 0)),
                     pl.BlockSpec((None, q, zw), lambda b, n: (b, ch(n), zoff)),
                     pl.BlockSpec((None, q, zw), lambda b, n: (b, ch(n), zoff + 1)),
                     pl.BlockSpec((1, wd), lambda b, n: (0, 0)),
                     pl.BlockSpec((1, wd), lambda b, n: (0, 0))]
        args += [y_other, proj, proj, d_row, norm_w.reshape(1, wd)]
    return pl.pallas_call(
        functools.partial(_ssd_kernel, rev=rev, final=final, nc=nc),
        grid=(bsz, nc),
        in_specs=in_specs,
        out_specs=pl.BlockSpec((None, q, wd), lambda b, n: (b, ch(n), 0)),
        out_shape=jax.ShapeDtypeStruct((bsz, l_len, wd), BF16 if final else F32),
        scratch_shapes=[pltpu.VMEM((wd, SSD_STATE), F32)],
        compiler_params=_params("parallel", "arbitrary"),
        name="ssd_bwd" if rev else "ssd_fwd",
    )(*args)


def _lane_params(*rows):
    out = jnp.zeros((8, LANE), F32)
    for i, r in enumerate(rows):
        r = r.astype(F32).reshape(-1)
        out = out.at[i, :r.shape[0]].set(r)
    return out


def _pad_cols(w, n):
    return jnp.pad(w, ((0, 0), (0, n - w.shape[1])))


def _diff_lambda_init(layer):
    return 0.8 - 0.6 * math.exp(-0.3 * layer)


def _tiles(l_pad):
    return _tile(l_pad, 1056, 16)


def _tall_tile(l_pad):
    return _tile(l_pad, 2112, 16)


def _even_mixer(h, hw, ssq, w_in, j, conv_w, a_log, dt_bias, gdn_norm_w, lam_vecs, diff_norm_w, w_out, lam_init,
                tables, next_norm_w):
    bsz, l_pad, d = h.shape
    tm = _tiles(l_pad)
    qkv_w = 3 * GDN_WIDTH
    main_w = qkv_w + GDN_WIDTH
    ab_w = 4 * GDN_HEADS
    wt = jnp.swapaxes(w_in, 1, 2)
    w_tail, w_ab = _tail_cast(wt, j, main_w, ab_w, wt.shape[1] - main_w - ab_w)
    tall = _tall_tile(l_pad)
    proj_a = _matmul([(hw, wt, j, 0, 0, True)], n=main_w, out_dtype=F32, tm=tall, tn=_tile(main_w, 512),
                     a_single=True, ssq=ssq, name="even_in_proj_a")
    proj_b = _matmul([(hw, w_tail, 0, 0, 0, True)], out_dtype=F32, tm=tall, tn=_tile(w_tail.shape[1], 512),
                     a_single=True, ssq=ssq, name="even_in_proj_b")
    ab = _matmul([(hw, w_ab, 0, 0, 0, True)], out_dtype=F32, tm=tm, tn=LANE, ssq=ssq, name="even_ab_proj")

    qkv_a = _conv_silu(proj_a, 0, qkv_w, conv_w, None)
    prm = _lane_params(a_log, dt_bias)
    o_bwd = _gdn_pass(qkv_a, ab, prm, rev=True)
    o_a = _gdn_pass(qkv_a, ab, prm, rev=False, extra=(o_bwd, proj_a, qkv_w, gdn_norm_w))

    qk_w = 2 * DIFF_HEADS * DIFF_DQK
    cols = ((0, qk_w), (qk_w, qk_w), (2 * qk_w, DIFF_WIDTH))
    qkv_b = _rope_cast(proj_b, cols, tables, tr=_tile(l_pad, 384, 16), q_scale=DIFF_DQK ** -0.5 * LOG2_E)
    o_b = _diff_attention(qkv_b, lam_vecs, diff_norm_w, lam_init, tq=_tile(l_pad, 384, 16))

    return _matmul([(o_a, w_out, j, 0, 0, False), (o_b, w_out, j, GDN_WIDTH, 0, False)], out_dtype=F32, tm=tm,
                   tn=_tile(d, 512), res=h, zero_pad=True, norm_w=next_norm_w, name="even_out_proj")


def _odd_mixer(h, hw, ssq, w_in, j, sink, conv_w, conv_b, a_log, dt_bias, d_skip, ssd_norm_w, w_out, tables,
               next_norm_w):
    bsz, l_pad, d = h.shape
    tm = _tiles(l_pad)
    kv_w = SWA_KV_HEADS * HEAD_DIM
    z_col = SWA_WIDTH + 2 * kv_w
    xbc_col = z_col + SSD_WIDTH
    main_w = xbc_col + SSD_XBC
    wt = jnp.swapaxes(w_in, 1, 2)
    w_dt = jnp.pad(wt[j, main_w:], ((0, LANE - (wt.shape[1] - main_w)), (0, 0))).astype(BF16)[None]
    proj = _matmul([(hw, wt, j, 0, 0, True)], n=main_w, out_dtype=F32, tm=_tall_tile(l_pad),
                   tn=_tile(main_w, 512), a_single=True, ssq=ssq, name="odd_in_proj")
    dt_raw = _matmul([(hw, w_dt, 0, 0, 0, True)], out_dtype=F32, tm=tm, tn=LANE, ssq=ssq, name="odd_dt_proj")

    cols = ((0, SWA_WIDTH), (SWA_WIDTH, kv_w), (SWA_WIDTH + kv_w, kv_w))
    qkv_c = _rope_cast(proj, cols, tables, tr=_tile(l_pad, 384, 16), q_scale=HEAD_DIM ** -0.5)
    o_c = _window_attention(qkv_c, sink.astype(F32))

    xbc = _conv_silu(proj, xbc_col, SSD_XBC, conv_w, conv_b)
    prm = _lane_params(a_log, dt_bias)
    d_row = jnp.repeat(d_skip.astype(F32), SSD_HEADDIM).reshape(1, SSD_WIDTH)
    y_bwd = _ssd_pass(xbc, dt_raw, prm, rev=True)
    y = _ssd_pass(xbc, dt_raw, prm, rev=False, extra=(y_bwd, proj, z_col, d_row, ssd_norm_w))

    return _matmul([(o_c, w_out, j, 0, 0, False), (y, w_out, j, SWA_WIDTH, 0, False)], out_dtype=F32, tm=tm,
                   tn=_tile(d, 512), res=h, zero_pad=True, norm_w=next_norm_w, name="odd_out_proj")


def _ffn(h, hw, ssq, w_gate, w_up, w_down_bf16, layer, next_norm_w):
    bsz, l_pad, d = h.shape
    f = w_gate.shape[2]
    act = _swiglu_up(hw, ssq, w_gate, w_up, layer, tm=_tall_tile(l_pad), tn=_tile(f, 256))
    return _matmul([(act, w_down_bf16, layer, 0, 0, False)], out_dtype=F32, tm=_tile(l_pad, 704, 16),
                   tn=_tile(d, 256), res=h, norm_w=next_norm_w, name="ffn_down")


def kernel(x, meta_tokens, norm_mix, norm_ffn, norm_final, even_w_in, even_conv, gdn_a_log, gdn_dt_bias, gdn_norm, diff_lam_q1, diff_lam_k1, diff_lam_q2, diff_lam_k2, diff_norm, even_w_out, odd_w_in, swa_sink, ssd_conv_w, ssd_conv_b, ssd_a_log, ssd_dt_bias, ssd_d, ssd_norm, odd_w_out, ffn_w_gate, ffn_w_up, ffn_w_down):
    bsz, seq, d = x.shape
    depth = norm_mix.shape[0]
    l_pad = LEAD + seq
    tables = _rope_tables(l_pad)
    w_down_bf16 = ffn_w_down.astype(BF16)
    h, hw, ssq = _embed_norm(x, meta_tokens, norm_mix[0])
    for i in range(depth):
        j = i // 2
        if i % 2 == 0:
            lam_vecs = jnp.stack([diff_lam_q1[j], diff_lam_k1[j], diff_lam_q2[j], diff_lam_k2[j]]).astype(F32)
            h, hw, ssq = _even_mixer(h, hw, ssq, even_w_in, j, even_conv[j], gdn_a_log[j], gdn_dt_bias[j],
                                     gdn_norm[j], lam_vecs, diff_norm[j], even_w_out, _diff_lambda_init(i), tables,
                                     norm_ffn[i])
        else:
            h, hw, ssq = _odd_mixer(h, hw, ssq, odd_w_in, j, swa_sink[j], ssd_conv_w[j], ssd_conv_b[j], ssd_a_log[j],
                                    ssd_dt_bias[j], ssd_d[j], ssd_norm[j], odd_w_out, tables, norm_ffn[i])
        if i + 1 < depth:
            h, hw, ssq = _ffn(h, hw, ssq, ffn_w_gate, ffn_w_up, w_down_bf16, i, norm_mix[i + 1])
        else:
            h = _ffn(h, hw, ssq, ffn_w_gate, ffn_w_up, w_down_bf16, i, None)
    return _rmsnorm(h, norm_final, out_dtype=x.dtype, row0=LEAD, rows=seq, tr=LANE)
```
